```python
import math, functools
import jax, jax.numpy as jnp
from jax import lax
import numpy as np

D_MODEL = 4096
BATCH = 4
SEQ = 2048
DEPTH = 1
DEC_BATCH = 128
DEC_SEQ = 8
PAST_LEN = 2048
PAGE_SIZE = 128

GLA_WIDTH = D_MODEL // 2
GLA_HEADS = 4
GLA_DV = GLA_WIDTH // GLA_HEADS
GLA_DK = GLA_DV // 2
GLA_KEY_WIDTH = GLA_HEADS * GLA_DK
GLA_GATE_RANK = 16
GLA_TAU = 16.0
GLA_CHUNK = 64
DIFF_WIDTH = D_MODEL - GLA_WIDTH
DIFF_HEAD_DIM = 128
DIFF_HEADS = DIFF_WIDTH // (2 * DIFF_HEAD_DIM)
DIFF_VDIM = 2 * DIFF_HEAD_DIM
DIFF_QK_WIDTH = DIFF_HEADS * 2 * DIFF_HEAD_DIM
Q_BLOCK = 128
D_FF = ((8 * D_MODEL + 3 * 256 - 1) // (3 * 256)) * 256
EPS = 1e-6
SPLIT_SIZES = (GLA_KEY_WIDTH, GLA_KEY_WIDTH, GLA_WIDTH, GLA_WIDTH, GLA_GATE_RANK,
               DIFF_QK_WIDTH, DIFF_QK_WIDTH, DIFF_HEADS * DIFF_VDIM)
D_IN = sum(SPLIT_SIZES)

kernel_name = "hybrid_gla_diffattn_step"


def _rms_norm(x, g):
    xf = x.astype(jnp.float32)
    y = xf * lax.rsqrt(jnp.mean(xf * xf, axis=-1, keepdims=True) + EPS)
    return (y * g.astype(jnp.float32)).astype(x.dtype)


def _gla_chunked(q, k, v, log_a, s0):
    B, T, H, _ = q.shape
    C = math.gcd(T, GLA_CHUNK)
    n = T // C

    def to_chunks(a):
        return a.reshape(B, n, C, *a.shape[2:]).swapaxes(0, 1)

    qc = to_chunks(q.astype(jnp.float32) * (GLA_DK ** -0.5))
    kc = to_chunks(k.astype(jnp.float32))
    vc = to_chunks(v.astype(jnp.float32))
    ac = to_chunks(log_a)
    mask = jnp.tril(jnp.ones((C, C), dtype=bool))

    def step(S, inp):
        qi, ki, vi, ai = inp
        b = jnp.cumsum(ai, axis=1)
        q_dec = qi * jnp.exp(b)
        k_inv = ki * jnp.exp(-b)
        scores = jnp.where(mask, jnp.einsum('bthk,bshk->bhts', q_dec, k_inv), 0.0)
        o = (jnp.einsum('bthk,bhkv->bthv', q_dec, S)
             + jnp.einsum('bhts,bshv->bthv', scores, vi))
        b_last = b[:, -1]
        k_carry = ki * jnp.exp(b_last[:, None] - b)
        S_new = S * jnp.exp(b_last)[..., None] + jnp.einsum('bshk,bshv->bhkv', k_carry, vi)
        return S_new, o

    S, o = lax.scan(step, s0.astype(jnp.float32), (qc, kc, vc, ac))
    return o.swapaxes(0, 1).reshape(B, T, H, -1), S


def _diff_attend_prompt(q, k, v, lam):
    B, S = q.shape[:2]
    nb = S // Q_BLOCK
    scale = DIFF_HEAD_DIM ** -0.5
    qb = q.reshape(B, nb, Q_BLOCK, DIFF_HEADS, 2, DIFF_HEAD_DIM).swapaxes(0, 1)
    kf = k.astype(jnp.float32)
    vf = v.astype(jnp.float32)
    kpos = jnp.arange(S)

    def block(args):
        qi, i = args
        s = jnp.einsum('bqhcd,bshcd->bhcqs', qi.astype(jnp.float32) * scale, kf)
        qpos = i * Q_BLOCK + jnp.arange(Q_BLOCK)
        s = jnp.where(kpos[None, :] <= qpos[:, None], s, -jnp.inf)
        p = jax.nn.softmax(s, axis=-1)
        o = jnp.einsum('bhcqs,bshv->bqhcv', p, vf)
        return o[:, :, :, 0] - lam * o[:, :, :, 1]

    o = lax.map(block, (qb, jnp.arange(nb)))
    return o.swapaxes(0, 1).reshape(B, S, DIFF_HEADS, DIFF_VDIM)


def _diff_attend_sample(q, k_new, v_new, lam, cache_k, cache_v, page_table, layer):
    T = q.shape[1]
    qf = q.astype(jnp.float32) * (DIFF_HEAD_DIM ** -0.5)
    s = jnp.einsum('bthcd,bshcd->bhcts', qf, k_new.astype(jnp.float32))
    s = jnp.where(jnp.tril(jnp.ones((T, T), dtype=bool)), s, -jnp.inf)
    m = jnp.max(s, axis=-1)
    p = jnp.exp(s - m[..., None])
    l = jnp.sum(p, axis=-1)
    acc = jnp.einsum('bhcts,bshv->bhctv', p, v_new.astype(jnp.float32))

    def page_step(carry, pages):
        m, l, acc = carry
        kp = cache_k[layer, pages].astype(jnp.float32)
        vp = cache_v[layer, pages].astype(jnp.float32)
        sp = jnp.einsum('bthcd,bphcd->bhctp', qf, kp)
        m_new = jnp.maximum(m, jnp.max(sp, axis=-1))
        corr = jnp.exp(m - m_new)
        pp = jnp.exp(sp - m_new[..., None])
        l = l * corr + jnp.sum(pp, axis=-1)
        acc = acc * corr[..., None] + jnp.einsum('bhctp,bphv->bhctv', pp, vp)
        return (m_new, l, acc), None

    (m, l, acc), _ = lax.scan(page_step, (m, l, acc), page_table.T)
    o = acc / l[..., None]
    o = o[:, :, 0] - lam * o[:, :, 1]
    return o.transpose(0, 2, 1, 3)


def _layer(x, gla_s0, attend, p, lam_init):
    B, T, _ = x.shape
    h = _rms_norm(x, p['pre_mix_norm'])
    z = h @ p['w_in']
    offs = np.cumsum(SPLIT_SIZES)[:-1].tolist()
    gq, gk, gv, gg, ga, dq, dk, dv = jnp.split(z, offs, axis=-1)
    log_a = jax.nn.log_sigmoid((ga @ p['w_a2'] + p['b_a']).astype(jnp.float32)) / GLA_TAU
    o_gla, S = _gla_chunked(gq.reshape(B, T, GLA_HEADS, GLA_DK),
                            gk.reshape(B, T, GLA_HEADS, GLA_DK),
                            gv.reshape(B, T, GLA_HEADS, GLA_DV),
                            log_a.reshape(B, T, GLA_HEADS, GLA_DK), gla_s0)
    o_gla = _rms_norm(o_gla, p['gla_norm']) * jax.nn.silu(
        gg.astype(jnp.float32).reshape(B, T, GLA_HEADS, GLA_DV))
    lam = (jnp.exp(jnp.sum(p['lambda_q1'].astype(jnp.float32) * p['lambda_k1'].astype(jnp.float32)))
           - jnp.exp(jnp.sum(p['lambda_q2'].astype(jnp.float32) * p['lambda_k2'].astype(jnp.float32)))
           + lam_init)
    dq = dq.reshape(B, T, DIFF_HEADS, 2, DIFF_HEAD_DIM)
    dk = dk.reshape(B, T, DIFF_HEADS, 2, DIFF_HEAD_DIM)
    dv = dv.reshape(B, T, DIFF_HEADS, DIFF_VDIM)
    o_diff = attend(dq, dk, dv, lam)
    o_diff = _rms_norm(o_diff, p['diff_norm']) * (1.0 - lam_init)
    mix = jnp.concatenate([o_gla.reshape(B, T, GLA_WIDTH),
                           o_diff.reshape(B, T, DIFF_WIDTH)], axis=-1).astype(x.dtype)
    x = x + _rms_norm(mix @ p['w_o'], p['post_mix_norm'])
    h = _rms_norm(x, p['pre_ffn_norm'])
    f = (jax.nn.silu(h @ p['w_gate']) * (h @ p['w_up'])) @ p['w_down']
    x = x + _rms_norm(f, p['post_ffn_norm'])
    return x, dk, dv, S


def setup_inputs(seed: int = 0) -> dict:
    key = jax.random.key(seed)
    ks = jax.random.split(key, 24)
    n_pages = PAST_LEN // PAGE_SIZE
    n_used = DEC_BATCH * n_pages
    n_pool = n_used + max(1, n_used // 4)
    f32 = jnp.float32

    def nrm(k, shape, scale):
        return jax.random.normal(k, shape, f32) * scale

    def gain(k, n):
        return 1.0 + 0.02 * jax.random.normal(k, (DEPTH, n), f32)

    page_table = jax.random.permutation(ks[5], n_pool)[:n_used].reshape(DEC_BATCH, n_pages).astype(jnp.int32)
    return {
        'x_prompt': nrm(ks[0], (BATCH, SEQ, D_MODEL), 1.0),
        'x_sample': nrm(ks[1], (DEC_BATCH, DEC_SEQ, D_MODEL), 1.0),
        'cache_k': nrm(ks[2], (DEPTH, n_pool, PAGE_SIZE, DIFF_HEADS, 2, DIFF_HEAD_DIM), 1.0),
        'cache_v': nrm(ks[3], (DEPTH, n_pool, PAGE_SIZE, DIFF_HEADS, DIFF_VDIM), 1.0),
        'state_gla': nrm(ks[4], (DEPTH, DEC_BATCH, GLA_HEADS, GLA_DK, GLA_DV), 0.5),
        'page_table': page_table,
        'pre_mix_norm': gain(ks[6], D_MODEL),
        'w_in': nrm(ks[7], (DEPTH, D_MODEL, D_IN), D_MODEL ** -0.5),
        'w_a2': nrm(ks[8], (DEPTH, GLA_GATE_RANK, GLA_KEY_WIDTH), GLA_GATE_RANK ** -0.5),
        'b_a': nrm(ks[9], (DEPTH, GLA_KEY_WIDTH), 0.1),
        'gla_norm': gain(ks[10], GLA_DV),
        'lambda_q1': nrm(ks[11], (DEPTH, DIFF_HEAD_DIM), 0.1),
        'lambda_k1': nrm(ks[12], (DEPTH, DIFF_HEAD_DIM), 0.1),
        'lambda_q2': nrm(ks[13], (DEPTH, DIFF_HEAD_DIM), 0.1),
        'lambda_k2': nrm(ks[14], (DEPTH, DIFF_HEAD_DIM), 0.1),
        'diff_norm': gain(ks[15], DIFF_VDIM),
        'w_o': nrm(ks[16], (DEPTH, D_MODEL, D_MODEL), D_MODEL ** -0.5),
        'post_mix_norm': gain(ks[17], D_MODEL),
        'pre_ffn_norm': gain(ks[18], D_MODEL),
        'w_gate': nrm(ks[19], (DEPTH, D_MODEL, D_FF), D_MODEL ** -0.5),
        'w_up': nrm(ks[20], (DEPTH, D_MODEL, D_FF), D_MODEL ** -0.5),
        'w_down': nrm(ks[21], (DEPTH, D_FF, D_MODEL), D_FF ** -0.5),
        'post_ffn_norm': gain(ks[22], D_MODEL),
    }


def reference(x_prompt, x_sample, cache_k, cache_v, state_gla, page_table,
              pre_mix_norm, w_in, w_a2, b_a, gla_norm, lambda_q1, lambda_k1, lambda_q2, lambda_k2,
              diff_norm, w_o, post_mix_norm, pre_ffn_norm, w_gate, w_up, w_down, post_ffn_norm):
    xp, xs = x_prompt, x_sample
    kp_l, vp_l, sp_l, ks_l, vs_l, ss_l = [], [], [], [], [], []
    for l in range(DEPTH):
        p = {
            'pre_mix_norm': pre_mix_norm[l], 'w_in': w_in[l], 'w_a2': w_a2[l], 'b_a': b_a[l],
            'gla_norm': gla_norm[l], 'lambda_q1': lambda_q1[l], 'lambda_k1': lambda_k1[l],
            'lambda_q2': lambda_q2[l], 'lambda_k2': lambda_k2[l], 'diff_norm': diff_norm[l],
            'w_o': w_o[l], 'post_mix_norm': post_mix_norm[l], 'pre_ffn_norm': pre_ffn_norm[l],
            'w_gate': w_gate[l], 'w_up': w_up[l], 'w_down': w_down[l], 'post_ffn_norm': post_ffn_norm[l],
        }
        lam_init = 0.8 - 0.6 * math.exp(-0.3 * l)
        s0 = jnp.zeros((xp.shape[0], GLA_HEADS, GLA_DK, GLA_DV), jnp.float32)
        xp, kp, vp, sp = _layer(xp, s0, _diff_attend_prompt, p, lam_init)
        sample_attend = functools.partial(_diff_attend_sample, cache_k=cache_k, cache_v=cache_v,
                                          page_table=page_table, layer=l)
        xs, ksm, vsm, ssm = _layer(xs, state_gla[l], sample_attend, p, lam_init)
        kp_l.append(kp); vp_l.append(vp); sp_l.append(sp)
        ks_l.append(ksm); vs_l.append(vsm); ss_l.append(ssm)
    k_prompt = jnp.stack(kp_l)
    v_prompt = jnp.stack(vp_l)
    gla_state_prompt = jnp.stack(sp_l)
    k_sample = jnp.stack(ks_l)
    v_sample = jnp.stack(vs_l)
    gla_state_sample = jnp.stack(ss_l)
    return (xp, xs, k_prompt, v_prompt, gla_state_prompt, k_sample, v_sample, gla_state_sample)
```

```python
import functools
import math

import jax
import jax.numpy as jnp
from jax import lax
from jax.experimental import pallas as pl
from jax.experimental.pallas import tpu as pltpu

F32 = jnp.float32
BF16 = jnp.bfloat16

D_MODEL = 4096
GLA_HEADS = 4
GLA_DK = 256
GLA_DV = 512
GLA_KEY_WIDTH = GLA_HEADS * GLA_DK
GLA_WIDTH = GLA_HEADS * GLA_DV
GLA_GATE_RANK = 16
GLA_TAU = 16.0
GLA_CHUNK = 64
DIFF_HEADS = 8
DIFF_HEAD_DIM = 128
DIFF_VDIM = 256
DIFF_WIDTH = DIFF_HEADS * DIFF_VDIM
PAGE_SIZE = 128
EPS = 1e-6

GLA_COLS = 2 * GLA_KEY_WIDTH + 2 * GLA_WIDTH
GATE_COL0 = GLA_COLS
DIFF_COL0 = GLA_COLS + GLA_GATE_RANK

LANES = 128
V7X_VMEM_BYTES = 64 * 1024 * 1024
VMEM_HEADROOM_BYTES = 6 * 1024 * 1024

ROW_TILE = 1024
NORM_ROWS = 256


def _cparams(semantics, vmem_bytes):
    limit = min(int(vmem_bytes) + VMEM_HEADROOM_BYTES, V7X_VMEM_BYTES - 2 * 1024 * 1024)
    return pltpu.CompilerParams(dimension_semantics=semantics, vmem_limit_bytes=limit)


def _rms(x, gain):
    ms = jnp.mean(x * x, axis=-1, keepdims=True)
    return x * lax.rsqrt(ms + EPS) * gain


def _dot(a, b):
    return jnp.dot(a, b, preferred_element_type=F32)


def _dot_nt(a, b):
    return lax.dot_general(a, b, (((1,), (1,)), ((), ())), preferred_element_type=F32)


def _dot_tn(a, b):
    return lax.dot_general(a, b, (((0,), (0,)), ((), ())), preferred_element_type=F32)


def _split_bf16(x):
    hi = x.astype(BF16)
    lo = (x - hi.astype(F32)).astype(BF16)
    return hi, lo


def _rmsnorm_body(x_ref, g_ref, o_ref):
    o_ref[...] = _rms(x_ref[...], g_ref[...]).astype(o_ref.dtype)


def _rmsnorm(x, gain, out_dtype):
    t, d = x.shape
    tr = NORM_ROWS
    vmem = 2 * tr * d * (4 + jnp.dtype(out_dtype).itemsize) + 4 * tr * d
    return pl.pallas_call(
        _rmsnorm_body,
        grid=(t // tr,),
        in_specs=[pl.BlockSpec((tr, d), lambda i: (i, 0)),
                  pl.BlockSpec((1, d), lambda i: (0, 0))],
        out_specs=pl.BlockSpec((tr, d), lambda i: (i, 0)),
        out_shape=jax.ShapeDtypeStruct((t, d), out_dtype),
        compiler_params=_cparams(("parallel",), vmem),
        name="rmsnorm",
    )(x, gain.reshape(1, d))


def _residual_norm2_body(x_ref, y_ref, g1_ref, g2_ref, x1_ref, h_ref):
    x1 = x_ref[...] + _rms(y_ref[...], g1_ref[...])
    x1_ref[...] = x1
    h_ref[...] = _rms(x1, g2_ref[...]).astype(h_ref.dtype)


def _residual_norm2(x, y, g1, g2):
    t, d = x.shape
    tr = NORM_ROWS
    vmem = 2 * tr * d * (4 + 4 + 4 + 2) + 8 * tr * d
    row = pl.BlockSpec((tr, d), lambda i: (i, 0))
    vec = pl.BlockSpec((1, d), lambda i: (0, 0))
    return pl.pallas_call(
        _residual_norm2_body,
        grid=(t // tr,),
        in_specs=[row, row, vec, vec],
        out_specs=[row, row],
        out_shape=[jax.ShapeDtypeStruct((t, d), F32), jax.ShapeDtypeStruct((t, d), BF16)],
        compiler_params=_cparams(("parallel",), vmem),
        name="residual_norm2",
    )(x, y, g1.reshape(1, d), g2.reshape(1, d))


def _residual_norm_body(x_ref, y_ref, g_ref, o_ref):
    o_ref[...] = x_ref[...] + _rms(y_ref[...], g_ref[...])


def _residual_norm(x, y, gain):
    t, d = x.shape
    tr = NORM_ROWS
    vmem = 2 * tr * d * 12 + 8 * tr * d
    row = pl.BlockSpec((tr, d), lambda i: (i, 0))
    vec = pl.BlockSpec((1, d), lambda i: (0, 0))
    return pl.pallas_call(
        _residual_norm_body,
        grid=(t // tr,),
        in_specs=[row, row, vec],
        out_specs=row,
        out_shape=jax.ShapeDtypeStruct((t, d), F32),
        compiler_params=_cparams(("parallel",), vmem),
        name="residual_norm",
    )(x, y, gain.reshape(1, d))


def _mm_body(a_ref, w_ref, o_ref):
    o_ref[...] = _dot(a_ref[...].astype(BF16), w_ref[...].astype(BF16)).astype(o_ref.dtype)


def _matmul(a, w, *, col0, n_out, tn, out_dtype):
    m, k = a.shape
    tm = min(ROW_TILE, m)
    assert m % tm == 0 and n_out % tn == 0 and col0 % tn == 0 and tn % LANES == 0
    cb0 = col0 // tn
    a_bytes = jnp.dtype(a.dtype).itemsize
    o_bytes = jnp.dtype(out_dtype).itemsize
    vmem = tm * k * a_bytes + 2 * k * tn * 4 + k * tn * 2 + 2 * tm * tn * o_bytes + tm * tn * 4
    return pl.pallas_call(
        _mm_body,
        grid=(m // tm, n_out // tn),
        in_specs=[pl.BlockSpec((tm, k), lambda i, j: (i, 0), pipeline_mode=pl.Buffered(1)),
                  pl.BlockSpec((k, tn), lambda i, j: (0, j + cb0))],
        out_specs=pl.BlockSpec((tm, tn), lambda i, j: (i, j)),
        out_shape=jax.ShapeDtypeStruct((m, n_out), out_dtype),
        compiler_params=_cparams(("parallel", "arbitrary"), vmem),
        name="proj",
    )(a, w)


def _mm2_body(a1_ref, a2_ref, w_ref, o_ref):
    k1 = a1_ref.shape[1]
    w = w_ref[...].astype(BF16)
    o_ref[...] = (_dot(a1_ref[...].astype(BF16), w[:k1]) + _dot(a2_ref[...].astype(BF16), w[k1:]))


def _matmul2(a1, a2, w, *, tn):
    m, k1 = a1.shape
    k2 = a2.shape[1]
    k, n = w.shape
    assert k == k1 + k2 and n % tn == 0
    tm = min(ROW_TILE, m)
    vmem = (tm * k1 * jnp.dtype(a1.dtype).itemsize + tm * k2 * jnp.dtype(a2.dtype).itemsize
            + 2 * k * tn * 4 + k * tn * 2 + 3 * tm * tn * 4)
    return pl.pallas_call(
        _mm2_body,
        grid=(m // tm, n // tn),
        in_specs=[pl.BlockSpec((tm, k1), lambda i, j: (i, 0), pipeline_mode=pl.Buffered(1)),
                  pl.BlockSpec((tm, k2), lambda i, j: (i, 0), pipeline_mode=pl.Buffered(1)),
                  pl.BlockSpec((k, tn), lambda i, j: (0, j))],
        out_specs=pl.BlockSpec((tm, tn), lambda i, j: (i, j)),
        out_shape=jax.ShapeDtypeStruct((m, n), F32),
        compiler_params=_cparams(("parallel", "arbitrary"), vmem),
        name="out_proj",
    )(a1, a2, w)


def _gate_up_body(a_ref, wg_ref, wu_ref, o_ref):
    a = a_ref[...]
    g = _dot(a, wg_ref[...].astype(BF16))
    u = _dot(a, wu_ref[...].astype(BF16))
    o_ref[...] = (g * (1.0 / (1.0 + jnp.exp(-g))) * u).astype(o_ref.dtype)


def _gate_up(a, w_gate, w_up, *, tn):
    m, k = a.shape
    n = w_gate.shape[1]
    assert n % tn == 0
    tm = min(ROW_TILE, m)
    vmem = tm * k * 2 + 4 * k * tn * 4 + 2 * k * tn * 2 + 2 * tm * tn * 2 + 3 * tm * tn * 4
    wspec = pl.BlockSpec((k, tn), lambda i, j: (0, j))
    return pl.pallas_call(
        _gate_up_body,
        grid=(m // tm, n // tn),
        in_specs=[pl.BlockSpec((tm, k), lambda i, j: (i, 0), pipeline_mode=pl.Buffered(1)), wspec, wspec],
        out_specs=pl.BlockSpec((tm, tn), lambda i, j: (i, j)),
        out_shape=jax.ShapeDtypeStruct((m, n), BF16),
        compiler_params=_cparams(("parallel", "arbitrary"), vmem),
        name="ffn_gate_up",
    )(a, w_gate, w_up)


def _log_sigmoid(x):
    return -(jnp.maximum(-x, 0.0) + jnp.log1p(jnp.exp(-jnp.abs(x))))


def _gla_chunk(q, k, v, la, s, tril_b, tril_mask):
    c = q.shape[0]
    la_hi, la_lo = _split_bf16(la)
    b = _dot(tril_b, la_hi) + _dot(tril_b, la_lo)
    b_last = b[c - 1:c, :]
    q_dec = (q * (GLA_DK ** -0.5) * jnp.exp(b)).astype(BF16)
    k_inv = (k * jnp.exp(-b)).astype(BF16)
    k_carry = k * jnp.exp(b_last - b)
    v_b = v.astype(BF16)
    scores = jnp.where(tril_mask, _dot_nt(q_dec, k_inv), 0.0).astype(BF16)
    o = _dot(q_dec, s.astype(BF16)) + _dot(scores, v_b)
    decay = jnp.exp(b_last)
    d_hi = decay.astype(BF16).astype(F32)
    d_mid = (decay - d_hi).astype(BF16).astype(F32)
    d_lo = (decay - d_hi - d_mid).astype(BF16).astype(F32)
    rows = lax.broadcasted_iota(jnp.int32, (16, GLA_DK), 0)
    split = jnp.where(rows == 0, d_hi, jnp.where(rows == 1, d_mid, jnp.where(rows == 2, d_lo, 0.0)))
    decay_mat = _dot_tn(split, jnp.ones((16, GLA_DV), F32))
    s_new = s * decay_mat + _dot_tn(k_carry, v)
    return o, s_new


def _gla_body(*refs, c_valid, n_chunks, n_seqs, has_state):
    if has_state:
        (q_ref, k_ref, v_ref, g_ref, ga_ref, wa_ref, ba_ref, gn_ref, s0_ref,
         o_ref, sout_ref, s_scr) = refs
    else:
        (q_ref, k_ref, v_ref, g_ref, ga_ref, wa_ref, ba_ref, gn_ref,
         o_ref, sout_ref, s_scr) = refs
        s0_ref = None
    blk = pl.program_id(2)
    cp = GLA_CHUNK

    @pl.when(blk == 0)
    def _():
        if has_state:
            s_scr[...] = s0_ref[:, 0]
        else:
            s_scr[...] = jnp.zeros(s_scr.shape, F32)

    wa_hi, wa_lo = _split_bf16(wa_ref[...])
    ba = ba_ref[...]
    gn = gn_ref[...]
    r = lax.broadcasted_iota(jnp.int32, (cp, cp), 0)
    cidx = lax.broadcasted_iota(jnp.int32, (cp, cp), 1)
    tril_mask = r >= cidx
    tril_b = jnp.where(tril_mask, 1.0, 0.0).astype(BF16)

    def pad(x):
        if c_valid == cp:
            return x
        return jnp.concatenate([x, jnp.zeros((cp - c_valid, x.shape[1]), F32)], axis=0)

    outs = []
    for sq in range(n_seqs):
        for ch in range(n_chunks):
            r0 = (sq * n_chunks + ch) * c_valid
            rows = slice(r0, r0 + c_valid)
            ga_hi, ga_lo = _split_bf16(ga_ref[rows, :])
            pre = _dot(ga_hi, wa_hi) + _dot(ga_hi, wa_lo) + _dot(ga_lo, wa_hi) + ba
            la = _log_sigmoid(pre) * (1.0 / GLA_TAU)
            o, s_new = _gla_chunk(pad(q_ref[rows, :]), pad(k_ref[rows, :]), pad(v_ref[rows, :]),
                                  pad(la), s_scr[sq], tril_b, tril_mask)
            s_scr[sq] = s_new
            o = o[:c_valid]
            gate = g_ref[rows, :]
            outs.append(_rms(o, gn) * (gate * (1.0 / (1.0 + jnp.exp(-gate)))))
    o_ref[...] = (outs[0] if len(outs) == 1 else jnp.concatenate(outs, axis=0)).astype(o_ref.dtype)

    @pl.when(blk == pl.num_programs(2) - 1)
    def _():
        sout_ref[:, 0] = s_scr[...]


def _gla(zg, zga, w_a2p, b_a, gla_norm, state0, *, batch, seq, c_valid, n_chunks, n_seqs):
    t = zg.shape[0]
    tb = n_seqs * n_chunks * c_valid
    assert (batch * seq) == t and (n_seqs * seq) % tb == 0
    nb = (n_seqs * seq) // tb
    bg = batch // n_seqs
    has_state = state0 is not None

    def rows(col):
        return lambda b, h, j: (b * nb + j, col(h))

    in_specs = [
        pl.BlockSpec((tb, GLA_DK), rows(lambda h: h)),
        pl.BlockSpec((tb, GLA_DK), rows(lambda h: GLA_HEADS + h)),
        pl.BlockSpec((tb, GLA_DV), rows(lambda h: GLA_HEADS + h)),
        pl.BlockSpec((tb, GLA_DV), rows(lambda h: 2 * GLA_HEADS + h)),
        pl.BlockSpec((tb, LANES), rows(lambda h: 0)),
        pl.BlockSpec((LANES, GLA_DK), lambda b, h, j: (0, h)),
        pl.BlockSpec((1, GLA_DK), lambda b, h, j: (0, h)),
        pl.BlockSpec((1, GLA_DV), lambda b, h, j: (0, 0)),
    ]
    args = [zg, zg, zg, zg, zga, w_a2p, b_a.reshape(1, -1), gla_norm.reshape(1, -1)]
    state_spec = pl.BlockSpec((n_seqs, 1, GLA_DK, GLA_DV), lambda b, h, j: (b, h, 0, 0))
    if has_state:
        in_specs.append(state_spec)
        args.append(state0)
    state_bytes = n_seqs * GLA_DK * GLA_DV * 4
    vmem = (2 * tb * (2 * GLA_DK + 2 * GLA_DV + LANES) * 4 + 2 * tb * GLA_DV * 2
            + (5 if has_state else 3) * state_bytes + 16 * 1024 * 1024)
    body = functools.partial(_gla_body, c_valid=c_valid, n_chunks=n_chunks, n_seqs=n_seqs,
                             has_state=has_state)
    return pl.pallas_call(
        body,
        grid=(bg, GLA_HEADS, nb),
        in_specs=in_specs,
        out_specs=[pl.BlockSpec((tb, GLA_DV), rows(lambda h: h)), state_spec],
        out_shape=[jax.ShapeDtypeStruct((t, GLA_WIDTH), BF16),
                   jax.ShapeDtypeStruct((batch, GLA_HEADS, GLA_DK, GLA_DV), F32)],
        scratch_shapes=[pltpu.VMEM((n_seqs, GLA_DK, GLA_DV), F32)],
        compiler_params=_cparams(("parallel", "parallel", "arbitrary"), vmem),
        name="gla",
    )(*args)


def _lambda(lq1_ref, lk1_ref, lq2_ref, lk2_ref, lam_init):
    a = jnp.sum(lq1_ref[...] * lk1_ref[...], axis=-1, keepdims=True)
    b = jnp.sum(lq2_ref[...] * lk2_ref[...], axis=-1, keepdims=True)
    return jnp.exp(a) - jnp.exp(b) + lam_init


def _online_softmax_step(s, v_b, m_ref, l_ref, acc_ref, idx):
    m_old = m_ref[idx]
    m_new = jnp.maximum(m_old, jnp.max(s, axis=-1, keepdims=True))
    corr = jnp.exp(m_old - m_new)
    p = jnp.exp(s - m_new[:, :1])
    l_ref[idx] = l_ref[idx] * corr + jnp.sum(p, axis=-1, keepdims=True)
    acc_ref[idx] = acc_ref[idx] * corr[:, :1] + _dot(p.astype(BF16), v_b)
    m_ref[idx] = m_new


def _diff_prompt_body(q_ref, k_ref, v_ref, lq1_ref, lk1_ref, lq2_ref, lk2_ref, dn_ref, o_ref,
                      m_scr, l_scr, acc_scr, *, tq, tk, lam_init):
    qi = pl.program_id(2)
    dh = DIFF_HEAD_DIM
    q = q_ref[...] * (dh ** -0.5)
    q_maps = (q[:, :dh].astype(BF16), q[:, dh:].astype(BF16))
    m_scr[...] = jnp.full(m_scr.shape, -jnp.inf, F32)
    l_scr[...] = jnp.zeros(l_scr.shape, F32)
    acc_scr[...] = jnp.zeros(acc_scr.shape, F32)
    qpos = qi * tq + lax.broadcasted_iota(jnp.int32, (tq, tk), 0)
    kcol = lax.broadcasted_iota(jnp.int32, (tq, tk), 1)
    n_kv = (qi * tq + tq + tk - 1) // tk

    def step(j, carry):
        k0 = pl.multiple_of(j * tk, tk)
        kb = k_ref[pl.ds(k0, tk), :]
        v_b = v_ref[pl.ds(k0, tk), :].astype(BF16)
        visible = (kcol + j * tk) <= qpos
        for c in range(2):
            kc = kb[:, c * dh:(c + 1) * dh].astype(BF16)
            s = jnp.where(visible, _dot_nt(q_maps[c], kc), -jnp.inf)
            _online_softmax_step(s, v_b, m_scr, l_scr, acc_scr, c)
        return carry

    lax.fori_loop(0, n_kv, step, 0)
    lam = _lambda(lq1_ref, lk1_ref, lq2_ref, lk2_ref, lam_init)
    o = acc_scr[0] / l_scr[0][:, :1] - lam * (acc_scr[1] / l_scr[1][:, :1])
    o_ref[...] = (_rms(o, dn_ref[...]) * (1.0 - lam_init)).astype(o_ref.dtype)


def _diff_prompt(zq, zk, zv, lams, diff_norm, *, batch, seq, lam_init, tq=256, tk=512):
    t = zq.shape[0]
    nq = seq // tq
    hw = 2 * DIFF_HEAD_DIM
    vec = pl.BlockSpec((1, DIFF_HEAD_DIM), lambda b, h, i: (0, 0))
    kv_spec = pl.BlockSpec((seq, hw), lambda b, h, i: (b, h))
    vmem = 2 * tq * hw * 4 + 4 * seq * hw * 4 + 2 * tq * hw * 2 + 4 * tq * LANES * 4 + 2 * tq * hw * 4 \
        + 8 * tq * tk * 4
    body = functools.partial(_diff_prompt_body, tq=tq, tk=tk, lam_init=lam_init)
    return pl.pallas_call(
        body,
        grid=(batch, DIFF_HEADS, nq),
        in_specs=[pl.BlockSpec((tq, hw), lambda b, h, i: (b * nq + i, h)), kv_spec, kv_spec,
                  vec, vec, vec, vec, pl.BlockSpec((1, DIFF_VDIM), lambda b, h, i: (0, 0))],
        out_specs=pl.BlockSpec((tq, DIFF_VDIM), lambda b, h, i: (b * nq + i, h)),
        out_shape=jax.ShapeDtypeStruct((t, DIFF_WIDTH), BF16),
        scratch_shapes=[pltpu.VMEM((2, tq, LANES), F32), pltpu.VMEM((2, tq, LANES), F32),
                        pltpu.VMEM((2, tq, DIFF_VDIM), F32)],
        compiler_params=_cparams(("parallel", "parallel", "arbitrary"), vmem),
        name="diff_attn_prompt",
    )(zq, zk, zv, *lams, diff_norm.reshape(1, -1))


def _diff_sample_body(pt_ref, q_ref, kn_ref, vn_ref, *rest, n_tok, pages_per_step, lam_init):
    kp_refs = rest[:pages_per_step]
    vp_refs = rest[pages_per_step:2 * pages_per_step]
    (lq1_ref, lk1_ref, lq2_ref, lk2_ref, dn_ref, o_ref,
     qbd_scr, m_scr, l_scr, acc_scr) = rest[2 * pages_per_step:]
    del pt_ref
    step = pl.program_id(1)
    dh = DIFF_HEAD_DIM
    hw = 2 * dh
    rows_h = 2 * n_tok
    n_rows = DIFF_HEADS * rows_h

    @pl.when(step == 0)
    def _():
        q = q_ref[...] * (dh ** -0.5)
        lane = lax.broadcasted_iota(jnp.int32, (n_tok, hw), 1)
        parts = []
        for h in range(DIFF_HEADS):
            qh = q[:, h * hw:(h + 1) * hw]
            parts += [jnp.where(lane < dh, qh, 0.0), jnp.where(lane >= dh, qh, 0.0)]
        qbd = jnp.concatenate(parts, axis=0).astype(BF16)
        qbd_scr[...] = qbd
        zpad = jnp.zeros((PAGE_SIZE - n_tok, hw), F32)
        s_parts, v_parts = [], []
        for h in range(DIFF_HEADS):
            kh = jnp.concatenate([kn_ref[:, h * hw:(h + 1) * hw], zpad], axis=0).astype(BF16)
            s_parts.append(_dot_nt(qbd[h * rows_h:(h + 1) * rows_h], kh))
            v_parts.append(jnp.concatenate([vn_ref[:, h * hw:(h + 1) * hw], zpad], axis=0).astype(BF16))
        s = jnp.concatenate(s_parts, axis=0)
        tok = jnp.bitwise_and(lax.broadcasted_iota(jnp.int32, (n_rows, PAGE_SIZE), 0), n_tok - 1)
        key = lax.broadcasted_iota(jnp.int32, (n_rows, PAGE_SIZE), 1)
        s = jnp.where(key <= tok, s, -jnp.inf)
        m = jnp.max(s, axis=-1, keepdims=True)
        p = jnp.exp(s - m)
        m_scr[...] = jnp.broadcast_to(m, m_scr.shape)
        l_scr[...] = jnp.broadcast_to(jnp.sum(p, axis=-1, keepdims=True), l_scr.shape)
        p_b = p.astype(BF16)
        for h in range(DIFF_HEADS):
            acc_scr[h * rows_h:(h + 1) * rows_h, :] = _dot(p_b[h * rows_h:(h + 1) * rows_h], v_parts[h])

    qbd = qbd_scr[...]
    s_pages = []
    for kp_ref in kp_refs:
        s_parts = []
        for h in range(DIFF_HEADS):
            kh = kp_ref[0, :, h * hw:(h + 1) * hw].astype(BF16)
            s_parts.append(_dot_nt(qbd[h * rows_h:(h + 1) * rows_h], kh))
        s_pages.append(jnp.concatenate(s_parts, axis=0))
    s = jnp.concatenate(s_pages, axis=1)
    m_old = m_scr[...]
    m_new = jnp.maximum(m_old, jnp.max(s, axis=-1, keepdims=True))
    corr = jnp.exp(m_old - m_new)
    p = jnp.exp(s - m_new[:, :1])
    l_scr[...] = l_scr[...] * corr + jnp.sum(p, axis=-1, keepdims=True)
    m_scr[...] = m_new
    p_b = p.astype(BF16)
    for h in range(DIFF_HEADS):
        hr = slice(h * rows_h, (h + 1) * rows_h)
        pv = None
        for i, vp_ref in enumerate(vp_refs):
            vh = vp_ref[0, :, h * hw:(h + 1) * hw].astype(BF16)
            d = _dot(p_b[hr, i * PAGE_SIZE:(i + 1) * PAGE_SIZE], vh)
            pv = d if pv is None else pv + d
        acc_scr[hr, :] = acc_scr[hr, :] * corr[hr, :1] + pv

    @pl.when(step == pl.num_programs(1) - 1)
    def _():
        lam = _lambda(lq1_ref, lk1_ref, lq2_ref, lk2_ref, lam_init)
        o_all = acc_scr[...] / l_scr[...][:, :1]
        dn = dn_ref[...]
        outs = []
        for h in range(DIFF_HEADS):
            o = o_all[h * rows_h:h * rows_h + n_tok] - lam * o_all[h * rows_h + n_tok:(h + 1) * rows_h]
            outs.append(_rms(o, dn) * (1.0 - lam_init))
        o_ref[...] = jnp.concatenate(outs, axis=1).astype(o_ref.dtype)


def _diff_sample(zq, zk, zv, cache_k, cache_v, page_table, lams, diff_norm, *, n_tok, lam_init,
                 pages_per_step=4):
    t = zq.shape[0]
    batch, n_pages = page_table.shape
    assert n_pages % pages_per_step == 0 and t == batch * n_tok
    assert n_tok & (n_tok - 1) == 0 and n_tok % 8 == 0
    n_steps = n_pages // pages_per_step
    hw = 2 * DIFF_HEAD_DIM
    n_rows = DIFF_HEADS * 2 * n_tok
    tok_spec = pl.BlockSpec((n_tok, DIFF_WIDTH), lambda b, s, pt: (b, 0))

    def page_spec(i):
        return pl.BlockSpec((1, PAGE_SIZE, DIFF_WIDTH),
                            lambda b, s, pt: (pt[b * n_pages + s * pages_per_step + i], 0, 0))

    vec = pl.BlockSpec((1, DIFF_HEAD_DIM), lambda b, s, pt: (0, 0))
    in_specs = ([tok_spec, tok_spec, tok_spec]
                + [page_spec(i) for i in range(pages_per_step)]
                + [page_spec(i) for i in range(pages_per_step)]
                + [vec, vec, vec, vec, pl.BlockSpec((1, DIFF_VDIM), lambda b, s, pt: (0, 0))])
    page_bytes = PAGE_SIZE * DIFF_WIDTH * 4
    vmem = 4 * pages_per_step * page_bytes + 2 * pages_per_step * page_bytes // 2 \
        + 8 * n_rows * pages_per_step * PAGE_SIZE * 4 + 8 * 1024 * 1024
    body = functools.partial(_diff_sample_body, n_tok=n_tok, pages_per_step=pages_per_step,
                             lam_init=lam_init)
    grid_spec = pltpu.PrefetchScalarGridSpec(
        num_scalar_prefetch=1,
        grid=(batch, n_steps),
        in_specs=in_specs,
        out_specs=pl.BlockSpec((n_tok, DIFF_WIDTH), lambda b, s, pt: (b, 0)),
        scratch_shapes=[pltpu.VMEM((n_rows, hw), BF16), pltpu.VMEM((n_rows, LANES), F32),
                        pltpu.VMEM((n_rows, LANES), F32), pltpu.VMEM((n_rows, DIFF_VDIM), F32)],
    )
    return pl.pallas_call(
        body,
        grid_spec=grid_spec,
        out_shape=jax.ShapeDtypeStruct((t, DIFF_WIDTH), F32),
        compiler_params=_cparams(("parallel", "arbitrary"), vmem),
        name="diff_attn_sample",
    )(page_table.reshape(-1), zq, zk, zv, *([cache_k] * pages_per_step), *([cache_v] * pages_per_step),
      *lams, diff_norm.reshape(1, -1))


def _layer(x, p, lam_init, *, batch, seq, gla_state, attend):
    h = _rmsnorm(x, p['pre_mix_norm'], BF16)
    zg = _matmul(h, p['w_in'], col0=0, n_out=GLA_COLS, tn=512, out_dtype=F32)
    zga = _matmul(h, p['w_gate_factor'], col0=0, n_out=LANES, tn=LANES, out_dtype=F32)
    zq = _matmul(h, p['w_diff'], col0=0, n_out=DIFF_WIDTH, tn=512, out_dtype=F32)
    zk = _matmul(h, p['w_diff'], col0=DIFF_WIDTH, n_out=DIFF_WIDTH, tn=512, out_dtype=F32)
    zv = _matmul(h, p['w_diff'], col0=2 * DIFF_WIDTH, n_out=DIFF_WIDTH, tn=512, out_dtype=F32)
    if gla_state is None:
        o_gla, s_out = _gla(zg, zga, p['w_a2p'], p['b_a'], p['gla_norm'], None, batch=batch, seq=seq,
                            c_valid=GLA_CHUNK, n_chunks=8, n_seqs=1)
    else:
        o_gla, s_out = _gla(zg, zga, p['w_a2p'], p['b_a'], p['gla_norm'], gla_state, batch=batch,
                            seq=seq, c_valid=seq, n_chunks=1, n_seqs=4)
    o_diff = attend(zq, zk, zv)
    mixed = _matmul2(o_gla, o_diff, p['w_o'], tn=512)
    x1, h2 = _residual_norm2(x, mixed, p['post_mix_norm'], p['pre_ffn_norm'])
    act = _gate_up(h2, p['w_gate'], p['w_up'], tn=256)
    f = _matmul(act, p['w_down'], col0=0, n_out=D_MODEL, tn=256, out_dtype=F32)
    y = _residual_norm(x1, f, p['post_ffn_norm'])
    return y, zk, zv, s_out


def kernel(x_prompt, x_sample, cache_k, cache_v, state_gla, page_table, pre_mix_norm, w_in, w_a2, b_a, gla_norm, lambda_q1, lambda_k1, lambda_q2, lambda_k2, diff_norm, w_o, post_mix_norm, pre_ffn_norm, w_gate, w_up, w_down, post_ffn_norm):
    depth = w_in.shape[0]
    bp, sp, d = x_prompt.shape
    bs, ss, _ = x_sample.shape
    n_pool = cache_k.shape[1]
    xp = x_prompt.reshape(bp * sp, d)
    xs = x_sample.reshape(bs * ss, d)
    outs = [[] for _ in range(6)]
    for l in range(depth):
        lam_init = 0.8 - 0.6 * math.exp(-0.3 * l)
        w_gate_factor = jnp.pad(w_in[l][:, GATE_COL0:GATE_COL0 + GLA_GATE_RANK],
                                ((0, 0), (0, LANES - GLA_GATE_RANK)))
        p = {
            'pre_mix_norm': pre_mix_norm[l], 'w_in': w_in[l], 'w_gate_factor': w_gate_factor,
            'w_diff': w_in[l][:, DIFF_COL0:],
            'w_a2p': jnp.pad(w_a2[l], ((0, LANES - GLA_GATE_RANK), (0, 0))),
            'b_a': b_a[l], 'gla_norm': gla_norm[l], 'w_o': w_o[l],
            'post_mix_norm': post_mix_norm[l], 'pre_ffn_norm': pre_ffn_norm[l],
            'w_gate': w_gate[l], 'w_up': w_up[l], 'w_down': w_down[l],
            'post_ffn_norm': post_ffn_norm[l],
        }
        lams = [v[l].reshape(1, -1) for v in (lambda_q1, lambda_k1, lambda_q2, lambda_k2)]
        ck = cache_k[l].reshape(n_pool, PAGE_SIZE, DIFF_WIDTH)
        cv = cache_v[l].reshape(n_pool, PAGE_SIZE, DIFF_WIDTH)
        attend_p = functools.partial(_diff_prompt, lams=lams, diff_norm=diff_norm[l], batch=bp, seq=sp,
                                     lam_init=lam_init)
        attend_s = functools.partial(_diff_sample, cache_k=ck, cache_v=cv, page_table=page_table,
                                     lams=lams, diff_norm=diff_norm[l], n_tok=ss, lam_init=lam_init)
        xp, kp, vp, s_p = _layer(xp, p, lam_init, batch=bp, seq=sp, gla_state=None, attend=attend_p)
        xs, ks, vs, s_s = _layer(xs, p, lam_init, batch=bs, seq=ss, gla_state=state_gla[l],
                                 attend=attend_s)
        outs[0].append(kp.reshape(bp, sp, DIFF_HEADS, 2, DIFF_HEAD_DIM))
        outs[1].append(vp.reshape(bp, sp, DIFF_HEADS, DIFF_VDIM))
        outs[2].append(s_p)
        outs[3].append(ks.reshape(bs, ss, DIFF_HEADS, 2, DIFF_HEAD_DIM))
        outs[4].append(vs.reshape(bs, ss, DIFF_HEADS, DIFF_VDIM))
        outs[5].append(s_s)
    k_p, v_p, st_p, k_s, v_s, st_s = [o[0][None] if depth == 1 else jnp.stack(o) for o in outs]
    return (xp.reshape(bp, sp, d), xs.reshape(bs, ss, d), k_p, v_p, st_p, k_s, v_s, st_s)
```

```python
import functools
import math

import jax
import jax.numpy as jnp
from jax import lax
from jax.experimental import pallas as pl
from jax.experimental.pallas import tpu as pltpu

F32 = jnp.float32
BF16 = jnp.bfloat16

D_MODEL = 4096
GLA_HEADS = 4
GLA_DK = 256
GLA_DV = 512
GLA_KEY_WIDTH = GLA_HEADS * GLA_DK
GLA_WIDTH = GLA_HEADS * GLA_DV
GLA_GATE_RANK = 16
GLA_TAU = 16.0
GLA_CHUNK = 64
DIFF_HEADS = 8
DIFF_HEAD_DIM = 128
DIFF_VDIM = 256
DIFF_WIDTH = DIFF_HEADS * DIFF_VDIM
PAGE_SIZE = 128
EPS = 1e-6

GLA_COLS = 2 * GLA_KEY_WIDTH + 2 * GLA_WIDTH
GATE_COL0 = GLA_COLS
DIFF_COL0 = GLA_COLS + GLA_GATE_RANK

LANES = 128
V7X_VMEM_BYTES = 64 * 1024 * 1024
VMEM_HEADROOM_BYTES = 6 * 1024 * 1024

ROW_TILE = 1024
NORM_ROWS = 256


def _cparams(semantics, vmem_bytes):
    limit = min(int(vmem_bytes) + VMEM_HEADROOM_BYTES, V7X_VMEM_BYTES - 2 * 1024 * 1024)
    return pltpu.CompilerParams(dimension_semantics=semantics, vmem_limit_bytes=limit)


def _rms(x, gain):
    ms = jnp.mean(x * x, axis=-1, keepdims=True)
    return x * lax.rsqrt(ms + EPS) * gain


def _dot(a, b):
    return jnp.dot(a, b, preferred_element_type=F32)


def _dot_nt(a, b):
    return lax.dot_general(a, b, (((1,), (1,)), ((), ())), preferred_element_type=F32)


def _dot_tn(a, b):
    return lax.dot_general(a, b, (((0,), (0,)), ((), ())), preferred_element_type=F32)


def _split_bf16(x):
    hi = x.astype(BF16)
    lo = (x - hi.astype(F32)).astype(BF16)
    return hi, lo


def _rmsnorm_body(x_ref, g_ref, o_ref):
    o_ref[...] = _rms(x_ref[...], g_ref[...]).astype(o_ref.dtype)


def _rmsnorm(x, gain, out_dtype):
    t, d = x.shape
    tr = NORM_ROWS
    vmem = 2 * tr * d * (4 + jnp.dtype(out_dtype).itemsize) + 4 * tr * d
    return pl.pallas_call(
        _rmsnorm_body,
        grid=(t // tr,),
        in_specs=[pl.BlockSpec((tr, d), lambda i: (i, 0)),
                  pl.BlockSpec((1, d), lambda i: (0, 0))],
        out_specs=pl.BlockSpec((tr, d), lambda i: (i, 0)),
        out_shape=jax.ShapeDtypeStruct((t, d), out_dtype),
        compiler_params=_cparams(("parallel",), vmem),
        name="rmsnorm",
    )(x, gain.reshape(1, d))


def _residual_norm2_body(x_ref, y_ref, g1_ref, g2_ref, x1_ref, h_ref):
    x1 = x_ref[...] + _rms(y_ref[...], g1_ref[...])
    x1_ref[...] = x1
    h_ref[...] = _rms(x1, g2_ref[...]).astype(h_ref.dtype)


def _residual_norm2(x, y, g1, g2):
    t, d = x.shape
    tr = NORM_ROWS
    vmem = 2 * tr * d * (4 + 4 + 4 + 2) + 8 * tr * d
    row = pl.BlockSpec((tr, d), lambda i: (i, 0))
    vec = pl.BlockSpec((1, d), lambda i: (0, 0))
    return pl.pallas_call(
        _residual_norm2_body,
        grid=(t // tr,),
        in_specs=[row, row, vec, vec],
        out_specs=[row, row],
        out_shape=[jax.ShapeDtypeStruct((t, d), F32), jax.ShapeDtypeStruct((t, d), BF16)],
        compiler_params=_cparams(("parallel",), vmem),
        name="residual_norm2",
    )(x, y, g1.reshape(1, d), g2.reshape(1, d))


def _residual_norm_body(x_ref, y_ref, g_ref, o_ref):
    o_ref[...] = x_ref[...] + _rms(y_ref[...], g_ref[...])


def _residual_norm(x, y, gain):
    t, d = x.shape
    tr = NORM_ROWS
    vmem = 2 * tr * d * 12 + 8 * tr * d
    row = pl.BlockSpec((tr, d), lambda i: (i, 0))
    vec = pl.BlockSpec((1, d), lambda i: (0, 0))
    return pl.pallas_call(
        _residual_norm_body,
        grid=(t // tr,),
        in_specs=[row, row, vec],
        out_specs=row,
        out_shape=jax.ShapeDtypeStruct((t, d), F32),
        compiler_params=_cparams(("parallel",), vmem),
        name="residual_norm",
    )(x, y, gain.reshape(1, d))


def _mm_body(a_ref, w_ref, o_ref):
    o_ref[...] = _dot(a_ref[...].astype(BF16), w_ref[...].astype(BF16)).astype(o_ref.dtype)


def _matmul(a, w, *, col0, n_out, tn, out_dtype):
    m, k = a.shape
    tm = min(ROW_TILE, m)
    assert m % tm == 0 and n_out % tn == 0 and col0 % tn == 0 and tn % LANES == 0
    cb0 = col0 // tn
    a_bytes = jnp.dtype(a.dtype).itemsize
    o_bytes = jnp.dtype(out_dtype).itemsize
    vmem = tm * k * a_bytes + 2 * k * tn * 4 + k * tn * 2 + 2 * tm * tn * o_bytes + tm * tn * 4
    return pl.pallas_call(
        _mm_body,
        grid=(m // tm, n_out // tn),
        in_specs=[pl.BlockSpec((tm, k), lambda i, j: (i, 0), pipeline_mode=pl.Buffered(1)),
                  pl.BlockSpec((k, tn), lambda i, j: (0, j + cb0))],
        out_specs=pl.BlockSpec((tm, tn), lambda i, j: (i, j)),
        out_shape=jax.ShapeDtypeStruct((m, n_out), out_dtype),
        compiler_params=_cparams(("parallel", "arbitrary"), vmem),
        name="proj",
    )(a, w)


def _mm_nt_body(a_ref, wt_ref, o_ref):
    o_ref[...] = _dot_nt(a_ref[...].astype(BF16), wt_ref[...].astype(BF16)).astype(o_ref.dtype)


def _matmul_nt(a, wt, *, row0, n_out, tn, out_dtype):
    m, k = a.shape
    tm = min(ROW_TILE, m)
    assert m % tm == 0 and n_out % tn == 0 and tn % LANES == 0 and row0 % 8 == 0
    assert wt.shape[1] == k and row0 + n_out <= wt.shape[0]
    a_bytes = jnp.dtype(a.dtype).itemsize
    o_bytes = jnp.dtype(out_dtype).itemsize
    vmem = tm * k * a_bytes + 2 * k * tn * 4 + k * tn * 2 + 2 * tm * tn * o_bytes + tm * tn * 4
    return pl.pallas_call(
        _mm_nt_body,
        grid=(m // tm, n_out // tn),
        in_specs=[pl.BlockSpec((tm, k), lambda i, j: (i, 0), pipeline_mode=pl.Buffered(1)),
                  pl.BlockSpec((pl.Element(tn), pl.Element(k)),
                               lambda i, j: (pl.multiple_of(row0 + j * tn, 8), 0))],
        out_specs=pl.BlockSpec((tm, tn), lambda i, j: (i, j)),
        out_shape=jax.ShapeDtypeStruct((m, n_out), out_dtype),
        compiler_params=_cparams(("parallel", "arbitrary"), vmem),
        name="proj_in",
    )(a, wt)


def _mm2_body(a1_ref, a2_ref, w_ref, o_ref):
    k1 = a1_ref.shape[1]
    w = w_ref[...].astype(BF16)
    o_ref[...] = (_dot(a1_ref[...].astype(BF16), w[:k1]) + _dot(a2_ref[...].astype(BF16), w[k1:]))


def _matmul2(a1, a2, w, *, tn):
    m, k1 = a1.shape
    k2 = a2.shape[1]
    k, n = w.shape
    assert k == k1 + k2 and n % tn == 0
    tm = min(ROW_TILE, m)
    vmem = (tm * k1 * jnp.dtype(a1.dtype).itemsize + tm * k2 * jnp.dtype(a2.dtype).itemsize
            + 2 * k * tn * 4 + k * tn * 2 + 3 * tm * tn * 4)
    return pl.pallas_call(
        _mm2_body,
        grid=(m // tm, n // tn),
        in_specs=[pl.BlockSpec((tm, k1), lambda i, j: (i, 0), pipeline_mode=pl.Buffered(1)),
                  pl.BlockSpec((tm, k2), lambda i, j: (i, 0), pipeline_mode=pl.Buffered(1)),
                  pl.BlockSpec((k, tn), lambda i, j: (0, j))],
        out_specs=pl.BlockSpec((tm, tn), lambda i, j: (i, j)),
        out_shape=jax.ShapeDtypeStruct((m, n), F32),
        compiler_params=_cparams(("parallel", "arbitrary"), vmem),
        name="out_proj",
    )(a1, a2, w)


def _gate_up_body(a_ref, wg_ref, wu_ref, o_ref):
    a = a_ref[...]
    g = _dot(a, wg_ref[...].astype(BF16))
    u = _dot(a, wu_ref[...].astype(BF16))
    o_ref[...] = (g * (1.0 / (1.0 + jnp.exp(-g))) * u).astype(o_ref.dtype)


def _gate_up(a, w_gate, w_up, *, tn):
    m, k = a.shape
    n = w_gate.shape[1]
    assert n % tn == 0
    tm = min(ROW_TILE, m)
    vmem = tm * k * 2 + 4 * k * tn * 4 + 2 * k * tn * 2 + 2 * tm * tn * 2 + 3 * tm * tn * 4
    wspec = pl.BlockSpec((k, tn), lambda i, j: (0, j))
    return pl.pallas_call(
        _gate_up_body,
        grid=(m // tm, n // tn),
        in_specs=[pl.BlockSpec((tm, k), lambda i, j: (i, 0), pipeline_mode=pl.Buffered(1)), wspec, wspec],
        out_specs=pl.BlockSpec((tm, tn), lambda i, j: (i, j)),
        out_shape=jax.ShapeDtypeStruct((m, n), BF16),
        compiler_params=_cparams(("parallel", "arbitrary"), vmem),
        name="ffn_gate_up",
    )(a, w_gate, w_up)


def _log_sigmoid(x):
    return -(jnp.maximum(-x, 0.0) + jnp.log1p(jnp.exp(-jnp.abs(x))))


def _gla_chunk(q, k, v, la, s, tril_b, tril_mask):
    c = q.shape[0]
    la_hi, la_lo = _split_bf16(la)
    b = _dot(tril_b, la_hi) + _dot(tril_b, la_lo)
    b_last = b[c - 1:c, :]
    q_dec = (q * (GLA_DK ** -0.5) * jnp.exp(b)).astype(BF16)
    k_inv = (k * jnp.exp(-b)).astype(BF16)
    k_carry = k * jnp.exp(b_last - b)
    v_b = v.astype(BF16)
    scores = jnp.where(tril_mask, _dot_nt(q_dec, k_inv), 0.0).astype(BF16)
    o = _dot(q_dec, s.astype(BF16)) + _dot(scores, v_b)
    decay = jnp.exp(b_last)
    d_hi = decay.astype(BF16).astype(F32)
    d_mid = (decay - d_hi).astype(BF16).astype(F32)
    d_lo = (decay - d_hi - d_mid).astype(BF16).astype(F32)
    rows = lax.broadcasted_iota(jnp.int32, (16, GLA_DK), 0)
    split = jnp.where(rows == 0, d_hi, jnp.where(rows == 1, d_mid, jnp.where(rows == 2, d_lo, 0.0)))
    decay_mat = _dot_tn(split, jnp.ones((16, GLA_DV), F32))
    s_new = s * decay_mat + _dot_tn(k_carry, v)
    return o, s_new


def _gla_body(*refs, c_valid, n_chunks, n_seqs, has_state):
    if has_state:
        (q_ref, k_ref, v_ref, g_ref, ga_ref, wa_ref, ba_ref, gn_ref, s0_ref,
         o_ref, sout_ref, s_scr) = refs
    else:
        (q_ref, k_ref, v_ref, g_ref, ga_ref, wa_ref, ba_ref, gn_ref,
         o_ref, sout_ref, s_scr) = refs
        s0_ref = None
    blk = pl.program_id(2)
    cp = GLA_CHUNK

    @pl.when(blk == 0)
    def _():
        if has_state:
            s_scr[...] = s0_ref[:, 0]
        else:
            s_scr[...] = jnp.zeros(s_scr.shape, F32)

    wa_hi, wa_lo = _split_bf16(wa_ref[...])
    ba = ba_ref[...]
    gn = gn_ref[...]
    r = lax.broadcasted_iota(jnp.int32, (cp, cp), 0)
    cidx = lax.broadcasted_iota(jnp.int32, (cp, cp), 1)
    tril_mask = r >= cidx
    tril_b = jnp.where(tril_mask, 1.0, 0.0).astype(BF16)

    def pad(x):
        if c_valid == cp:
            return x
        return jnp.concatenate([x, jnp.zeros((cp - c_valid, x.shape[1]), F32)], axis=0)

    outs = []
    for sq in range(n_seqs):
        for ch in range(n_chunks):
            r0 = (sq * n_chunks + ch) * c_valid
            rows = slice(r0, r0 + c_valid)
            ga_hi, ga_lo = _split_bf16(ga_ref[rows, :])
            pre = _dot(ga_hi, wa_hi) + _dot(ga_hi, wa_lo) + _dot(ga_lo, wa_hi) + ba
            la = _log_sigmoid(pre) * (1.0 / GLA_TAU)
            o, s_new = _gla_chunk(pad(q_ref[rows, :]), pad(k_ref[rows, :]), pad(v_ref[rows, :]),
                                  pad(la), s_scr[sq], tril_b, tril_mask)
            s_scr[sq] = s_new
            o = o[:c_valid]
            gate = g_ref[rows, :]
            outs.append(_rms(o, gn) * (gate * (1.0 / (1.0 + jnp.exp(-gate)))))
    o_ref[...] = (outs[0] if len(outs) == 1 else jnp.concatenate(outs, axis=0)).astype(o_ref.dtype)

    @pl.when(blk == pl.num_programs(2) - 1)
    def _():
        sout_ref[:, 0] = s_scr[...]


def _gla(zg, zga, w_a2p, b_a, gla_norm, state0, *, batch, seq, c_valid, n_chunks, n_seqs):
    t = zg.shape[0]
    tb = n_seqs * n_chunks * c_valid
    assert (batch * seq) == t and (n_seqs * seq) % tb == 0
    nb = (n_seqs * seq) // tb
    bg = batch // n_seqs
    has_state = state0 is not None

    def rows(col):
        return lambda b, h, j: (b * nb + j, col(h))

    in_specs = [
        pl.BlockSpec((tb, GLA_DK), rows(lambda h: h)),
        pl.BlockSpec((tb, GLA_DK), rows(lambda h: GLA_HEADS + h)),
        pl.BlockSpec((tb, GLA_DV), rows(lambda h: GLA_HEADS + h)),
        pl.BlockSpec((tb, GLA_DV), rows(lambda h: 2 * GLA_HEADS + h)),
        pl.BlockSpec((tb, LANES), rows(lambda h: 0)),
        pl.BlockSpec((LANES, GLA_DK), lambda b, h, j: (0, h)),
        pl.BlockSpec((1, GLA_DK), lambda b, h, j: (0, h)),
        pl.BlockSpec((1, GLA_DV), lambda b, h, j: (0, 0)),
    ]
    args = [zg, zg, zg, zg, zga, w_a2p, b_a.reshape(1, -1), gla_norm.reshape(1, -1)]
    state_spec = pl.BlockSpec((n_seqs, 1, GLA_DK, GLA_DV), lambda b, h, j: (b, h, 0, 0))
    if has_state:
        in_specs.append(state_spec)
        args.append(state0)
    state_bytes = n_seqs * GLA_DK * GLA_DV * 4
    vmem = (2 * tb * (2 * GLA_DK + 2 * GLA_DV + LANES) * 4 + 2 * tb * GLA_DV * 2
            + (5 if has_state else 3) * state_bytes + 16 * 1024 * 1024)
    body = functools.partial(_gla_body, c_valid=c_valid, n_chunks=n_chunks, n_seqs=n_seqs,
                             has_state=has_state)
    return pl.pallas_call(
        body,
        grid=(bg, GLA_HEADS, nb),
        in_specs=in_specs,
        out_specs=[pl.BlockSpec((tb, GLA_DV), rows(lambda h: h)), state_spec],
        out_shape=[jax.ShapeDtypeStruct((t, GLA_WIDTH), BF16),
                   jax.ShapeDtypeStruct((batch, GLA_HEADS, GLA_DK, GLA_DV), F32)],
        scratch_shapes=[pltpu.VMEM((n_seqs, GLA_DK, GLA_DV), F32)],
        compiler_params=_cparams(("parallel", "parallel", "arbitrary"), vmem),
        name="gla",
    )(*args)


def _lambda(lq1_ref, lk1_ref, lq2_ref, lk2_ref, lam_init):
    a = jnp.sum(lq1_ref[...] * lk1_ref[...], axis=-1, keepdims=True)
    b = jnp.sum(lq2_ref[...] * lk2_ref[...], axis=-1, keepdims=True)
    return jnp.exp(a) - jnp.exp(b) + lam_init


def _online_softmax_step(s, v_b, m_ref, l_ref, acc_ref, idx):
    m_old = m_ref[idx]
    m_new = jnp.maximum(m_old, jnp.max(s, axis=-1, keepdims=True))
    corr = jnp.exp(m_old - m_new)
    p = jnp.exp(s - m_new[:, :1])
    l_ref[idx] = l_ref[idx] * corr + jnp.sum(p, axis=-1, keepdims=True)
    acc_ref[idx] = acc_ref[idx] * corr[:, :1] + _dot(p.astype(BF16), v_b)
    m_ref[idx] = m_new


def _diff_prompt_body(q_ref, k_ref, v_ref, lq1_ref, lk1_ref, lq2_ref, lk2_ref, dn_ref, o_ref,
                      m_scr, l_scr, acc_scr, *, tq, tk, lam_init):
    qi = pl.program_id(2)
    dh = DIFF_HEAD_DIM
    q = q_ref[...] * (dh ** -0.5)
    q_maps = (q[:, :dh].astype(BF16), q[:, dh:].astype(BF16))
    m_scr[...] = jnp.full(m_scr.shape, -jnp.inf, F32)
    l_scr[...] = jnp.zeros(l_scr.shape, F32)
    acc_scr[...] = jnp.zeros(acc_scr.shape, F32)
    qpos = qi * tq + lax.broadcasted_iota(jnp.int32, (tq, tk), 0)
    kcol = lax.broadcasted_iota(jnp.int32, (tq, tk), 1)
    n_kv = (qi * tq + tq + tk - 1) // tk

    def step(j, carry):
        k0 = pl.multiple_of(j * tk, tk)
        kb = k_ref[pl.ds(k0, tk), :]
        v_b = v_ref[pl.ds(k0, tk), :].astype(BF16)
        visible = (kcol + j * tk) <= qpos
        for c in range(2):
            kc = kb[:, c * dh:(c + 1) * dh].astype(BF16)
            s = jnp.where(visible, _dot_nt(q_maps[c], kc), -jnp.inf)
            _online_softmax_step(s, v_b, m_scr, l_scr, acc_scr, c)
        return carry

    lax.fori_loop(0, n_kv, step, 0)
    lam = _lambda(lq1_ref, lk1_ref, lq2_ref, lk2_ref, lam_init)
    o = acc_scr[0] / l_scr[0][:, :1] - lam * (acc_scr[1] / l_scr[1][:, :1])
    o_ref[...] = (_rms(o, dn_ref[...]) * (1.0 - lam_init)).astype(o_ref.dtype)


def _diff_prompt(zq, zk, zv, lams, diff_norm, *, batch, seq, lam_init, tq=512, tk=512):
    t = zq.shape[0]
    nq = seq // tq
    hw = 2 * DIFF_HEAD_DIM
    vec = pl.BlockSpec((1, DIFF_HEAD_DIM), lambda b, h, i: (0, 0))
    kv_spec = pl.BlockSpec((seq, hw), lambda b, h, i: (b, h))
    vmem = 2 * tq * hw * 4 + 4 * seq * hw * 4 + 2 * tq * hw * 2 + 4 * tq * LANES * 4 + 2 * tq * hw * 4 \
        + 8 * tq * tk * 4
    body = functools.partial(_diff_prompt_body, tq=tq, tk=tk, lam_init=lam_init)
    return pl.pallas_call(
        body,
        grid=(batch, DIFF_HEADS, nq),
        in_specs=[pl.BlockSpec((tq, hw), lambda b, h, i: (b * nq + i, h)), kv_spec, kv_spec,
                  vec, vec, vec, vec, pl.BlockSpec((1, DIFF_VDIM), lambda b, h, i: (0, 0))],
        out_specs=pl.BlockSpec((tq, DIFF_VDIM), lambda b, h, i: (b * nq + i, h)),
        out_shape=jax.ShapeDtypeStruct((t, DIFF_WIDTH), BF16),
        scratch_shapes=[pltpu.VMEM((2, tq, LANES), F32), pltpu.VMEM((2, tq, LANES), F32),
                        pltpu.VMEM((2, tq, DIFF_VDIM), F32)],
        compiler_params=_cparams(("parallel", "parallel", "arbitrary"), vmem),
        name="diff_attn_prompt",
    )(zq, zk, zv, *lams, diff_norm.reshape(1, -1))


def _diff_sample_body(pt_ref, q_ref, kn_ref, vn_ref, *rest, n_tok, pages_per_step, lam_init):
    kp_refs = rest[:pages_per_step]
    vp_refs = rest[pages_per_step:2 * pages_per_step]
    (lq1_ref, lk1_ref, lq2_ref, lk2_ref, dn_ref, o_ref,
     q_scr, m_scr, l_scr, acc_scr) = rest[2 * pages_per_step:]
    del pt_ref
    step = pl.program_id(1)
    dh = DIFF_HEAD_DIM
    hw = 2 * dh
    n_half = DIFF_HEADS * n_tok
    page_rows = PAGE_SIZE * DIFF_HEADS
    tok_bits = n_tok.bit_length() - 1

    def row_head(r):
        return jnp.bitwise_and(jnp.right_shift(r, tok_bits), DIFF_HEADS - 1)

    @pl.when(step == 0)
    def _():
        q = q_ref[...] * (dh ** -0.5)
        kn = kn_ref[...]
        vn = vn_ref[...]
        zpad_k = jnp.zeros((PAGE_SIZE - n_half, dh), F32)
        s_parts = []
        for c in range(2):
            qc = jnp.concatenate([q[:, h * hw + c * dh:h * hw + (c + 1) * dh] for h in range(DIFF_HEADS)],
                                 axis=0).astype(BF16)
            q_scr[c] = qc
            knc = jnp.concatenate([kn[:, h * hw + c * dh:h * hw + (c + 1) * dh] for h in range(DIFF_HEADS)]
                                  + [zpad_k], axis=0).astype(BF16)
            s_parts.append(_dot_nt(qc, knc))
        s = jnp.concatenate(s_parts, axis=0)
        r = lax.broadcasted_iota(jnp.int32, s.shape, 0)
        col = lax.broadcasted_iota(jnp.int32, s.shape, 1)
        visible = jnp.logical_and(jnp.right_shift(col, tok_bits) == row_head(r),
                                  jnp.bitwise_and(col, n_tok - 1) <= jnp.bitwise_and(r, n_tok - 1))
        s = jnp.where(visible, s, -jnp.inf)
        m = jnp.max(s, axis=-1, keepdims=True)
        p = jnp.exp(s - m)
        m_scr[...] = jnp.broadcast_to(m, m_scr.shape)
        l_scr[...] = jnp.broadcast_to(jnp.sum(p, axis=-1, keepdims=True), l_scr.shape)
        vn_rows = jnp.concatenate([vn[:, h * hw:(h + 1) * hw] for h in range(DIFF_HEADS)]
                                  + [jnp.zeros((PAGE_SIZE - n_half, hw), F32)], axis=0).astype(BF16)
        acc_scr[...] = _dot(p.astype(BF16), vn_rows)

    q0 = q_scr[0]
    q1 = q_scr[1]
    s_pages = []
    for kp_ref in kp_refs:
        k0 = kp_ref[0, pl.ds(0, page_rows, stride=2), :].astype(BF16)
        k1 = kp_ref[0, pl.ds(1, page_rows, stride=2), :].astype(BF16)
        s_pages.append(jnp.concatenate([_dot_nt(q0, k0), _dot_nt(q1, k1)], axis=0))
    s = jnp.concatenate(s_pages, axis=1)
    r = lax.broadcasted_iota(jnp.int32, s.shape, 0)
    col = lax.broadcasted_iota(jnp.int32, s.shape, 1)
    s = jnp.where(jnp.bitwise_and(col, DIFF_HEADS - 1) == row_head(r), s, -jnp.inf)
    m_old = m_scr[...]
    m_new = jnp.maximum(m_old, jnp.max(s, axis=-1, keepdims=True))
    corr = jnp.exp(m_old - m_new)
    p = jnp.exp(s - m_new[:, :1])
    l_scr[...] = l_scr[...] * corr + jnp.sum(p, axis=-1, keepdims=True)
    m_scr[...] = m_new
    p_b = p.astype(BF16)
    pv = None
    for i, vp_ref in enumerate(vp_refs):
        d = _dot(p_b[:, i * page_rows:(i + 1) * page_rows], vp_ref[0].astype(BF16))
        pv = d if pv is None else pv + d
    acc_scr[...] = acc_scr[...] * corr[:, :1] + pv

    @pl.when(step == pl.num_programs(1) - 1)
    def _():
        lam = _lambda(lq1_ref, lk1_ref, lq2_ref, lk2_ref, lam_init)
        o_all = acc_scr[...] / l_scr[...][:, :1]
        dn = dn_ref[...]
        outs = []
        for h in range(DIFF_HEADS):
            o = o_all[h * n_tok:(h + 1) * n_tok] - lam * o_all[n_half + h * n_tok:n_half + (h + 1) * n_tok]
            outs.append(_rms(o, dn) * (1.0 - lam_init))
        o_ref[...] = jnp.concatenate(outs, axis=1).astype(o_ref.dtype)


def _diff_sample(zq, zk, zv, cache_k, cache_v, page_table, lams, diff_norm, *, n_tok, lam_init,
                 pages_per_step=4):
    t = zq.shape[0]
    batch, n_pages = page_table.shape
    assert n_pages % pages_per_step == 0 and t == batch * n_tok
    assert n_tok & (n_tok - 1) == 0 and n_tok % 8 == 0 and DIFF_HEADS * n_tok <= PAGE_SIZE
    n_steps = n_pages // pages_per_step
    dh = DIFF_HEAD_DIM
    hw = 2 * dh
    n_half = DIFF_HEADS * n_tok
    n_rows = 2 * n_half
    page_rows = PAGE_SIZE * DIFF_HEADS
    tok_spec = pl.BlockSpec((n_tok, DIFF_WIDTH), lambda b, s, pt: (b, 0))

    def page_spec(shape, i):
        return pl.BlockSpec((1,) + shape,
                            lambda b, s, pt: (pt[b * n_pages + s * pages_per_step + i], 0, 0))

    vec = pl.BlockSpec((1, dh), lambda b, s, pt: (0, 0))
    in_specs = ([tok_spec, tok_spec, tok_spec]
                + [page_spec((2 * page_rows, dh), i) for i in range(pages_per_step)]
                + [page_spec((page_rows, hw), i) for i in range(pages_per_step)]
                + [vec, vec, vec, vec, pl.BlockSpec((1, DIFF_VDIM), lambda b, s, pt: (0, 0))])
    page_bytes = page_rows * hw * 4
    vmem = 4 * pages_per_step * page_bytes + 2 * pages_per_step * page_bytes \
        + 6 * n_rows * pages_per_step * page_rows * 4 + 4 * 1024 * 1024
    body = functools.partial(_diff_sample_body, n_tok=n_tok, pages_per_step=pages_per_step,
                             lam_init=lam_init)
    grid_spec = pltpu.PrefetchScalarGridSpec(
        num_scalar_prefetch=1,
        grid=(batch, n_steps),
        in_specs=in_specs,
        out_specs=pl.BlockSpec((n_tok, DIFF_WIDTH), lambda b, s, pt: (b, 0)),
        scratch_shapes=[pltpu.VMEM((2, n_half, dh), BF16), pltpu.VMEM((n_rows, LANES), F32),
                        pltpu.VMEM((n_rows, LANES), F32), pltpu.VMEM((n_rows, DIFF_VDIM), F32)],
    )
    return pl.pallas_call(
        body,
        grid_spec=grid_spec,
        out_shape=jax.ShapeDtypeStruct((t, DIFF_WIDTH), F32),
        compiler_params=_cparams(("parallel", "arbitrary"), vmem),
        name="diff_attn_sample",
    )(page_table.reshape(-1), zq, zk, zv, *([cache_k] * pages_per_step), *([cache_v] * pages_per_step),
      *lams, diff_norm.reshape(1, -1))


def _layer(x, p, lam_init, *, batch, seq, gla_state, attend):
    h = _rmsnorm(x, p['pre_mix_norm'], BF16)
    wt = p['w_in_t']
    zg = _matmul_nt(h, wt, row0=0, n_out=GLA_COLS, tn=512, out_dtype=F32)
    zga = _matmul_nt(h, wt, row0=GATE_COL0, n_out=LANES, tn=LANES, out_dtype=F32)
    zq = _matmul_nt(h, wt, row0=DIFF_COL0, n_out=DIFF_WIDTH, tn=512, out_dtype=F32)
    zk = _matmul_nt(h, wt, row0=DIFF_COL0 + DIFF_WIDTH, n_out=DIFF_WIDTH, tn=512, out_dtype=F32)
    zv = _matmul_nt(h, wt, row0=DIFF_COL0 + 2 * DIFF_WIDTH, n_out=DIFF_WIDTH, tn=512, out_dtype=F32)
    if gla_state is None:
        o_gla, s_out = _gla(zg, zga, p['w_a2p'], p['b_a'], p['gla_norm'], None, batch=batch, seq=seq,
                            c_valid=GLA_CHUNK, n_chunks=8, n_seqs=1)
    else:
        o_gla, s_out = _gla(zg, zga, p['w_a2p'], p['b_a'], p['gla_norm'], gla_state, batch=batch,
                            seq=seq, c_valid=seq, n_chunks=1, n_seqs=4)
    o_diff = attend(zq, zk, zv)
    mixed = _matmul2(o_gla, o_diff, p['w_o'], tn=512)
    x1, h2 = _residual_norm2(x, mixed, p['post_mix_norm'], p['pre_ffn_norm'])
    act = _gate_up(h2, p['w_gate'], p['w_up'], tn=256)
    f = _matmul(act, p['w_down'], col0=0, n_out=D_MODEL, tn=256, out_dtype=F32)
    y = _residual_norm(x1, f, p['post_ffn_norm'])
    return y, zk, zv, s_out


def kernel(x_prompt, x_sample, cache_k, cache_v, state_gla, page_table, pre_mix_norm, w_in, w_a2, b_a, gla_norm, lambda_q1, lambda_k1, lambda_q2, lambda_k2, diff_norm, w_o, post_mix_norm, pre_ffn_norm, w_gate, w_up, w_down, post_ffn_norm):
    depth = w_in.shape[0]
    bp, sp, d = x_prompt.shape
    bs, ss, _ = x_sample.shape
    n_pool = cache_k.shape[1]
    xp = x_prompt.reshape(bp * sp, d)
    xs = x_sample.reshape(bs * ss, d)
    outs = [[] for _ in range(6)]
    for l in range(depth):
        lam_init = 0.8 - 0.6 * math.exp(-0.3 * l)
        p = {
            'pre_mix_norm': pre_mix_norm[l], 'w_in_t': jnp.swapaxes(w_in[l], 0, 1),
            'w_a2p': jnp.pad(w_a2[l], ((0, LANES - GLA_GATE_RANK), (0, 0))),
            'b_a': b_a[l], 'gla_norm': gla_norm[l], 'w_o': w_o[l],
            'post_mix_norm': post_mix_norm[l], 'pre_ffn_norm': pre_ffn_norm[l],
            'w_gate': w_gate[l], 'w_up': w_up[l], 'w_down': w_down[l],
            'post_ffn_norm': post_ffn_norm[l],
        }
        lams = [v[l].reshape(1, -1) for v in (lambda_q1, lambda_k1, lambda_q2, lambda_k2)]
        ck = cache_k[l].reshape(n_pool, 2 * PAGE_SIZE * DIFF_HEADS, DIFF_HEAD_DIM)
        cv = cache_v[l].reshape(n_pool, PAGE_SIZE * DIFF_HEADS, DIFF_VDIM)
        attend_p = functools.partial(_diff_prompt, lams=lams, diff_norm=diff_norm[l], batch=bp, seq=sp,
                                     lam_init=lam_init)
        attend_s = functools.partial(_diff_sample, cache_k=ck, cache_v=cv, page_table=page_table,
                                     lams=lams, diff_norm=diff_norm[l], n_tok=ss, lam_init=lam_init)
        xp, kp, vp, s_p = _layer(xp, p, lam_init, batch=bp, seq=sp, gla_state=None, attend=attend_p)
        xs, ks, vs, s_s = _layer(xs, p, lam_init, batch=bs, seq=ss, gla_state=state_gla[l],
                                 attend=attend_s)
        outs[0].append(kp.reshape(bp, sp, DIFF_HEADS, 2, DIFF_HEAD_DIM))
        outs[1].append(vp.reshape(bp, sp, DIFF_HEADS, DIFF_VDIM))
        outs[2].append(s_p)
        outs[3].append(ks.reshape(bs, ss, DIFF_HEADS, 2, DIFF_HEAD_DIM))
        outs[4].append(vs.reshape(bs, ss, DIFF_HEADS, DIFF_VDIM))
        outs[5].append(s_s)
    k_p, v_p, st_p, k_s, v_s, st_s = [o[0][None] if depth == 1 else jnp.stack(o) for o in outs]
    return (xp.reshape(bp, sp, d), xs.reshape(bs, ss, d), k_p, v_p, st_p, k_s, v_s, st_s)
```

```python
import functools
import math

import jax
import jax.numpy as jnp
from jax import lax
from jax.experimental import pallas as pl
from jax.experimental.pallas import tpu as pltpu

F32 = jnp.float32
BF16 = jnp.bfloat16

D_MODEL = 4096
GLA_HEADS = 4
GLA_DK = 256
GLA_DV = 512
GLA_KEY_WIDTH = GLA_HEADS * GLA_DK
GLA_WIDTH = GLA_HEADS * GLA_DV
GLA_GATE_RANK = 16
GLA_TAU = 16.0
GLA_CHUNK = 64
DIFF_HEADS = 8
DIFF_HEAD_DIM = 128
DIFF_VDIM = 256
DIFF_WIDTH = DIFF_HEADS * DIFF_VDIM
PAGE_SIZE = 128
EPS = 1e-6

GLA_COLS = 2 * GLA_KEY_WIDTH + 2 * GLA_WIDTH
GATE_COL0 = GLA_COLS
DIFF_COL0 = GLA_COLS + GLA_GATE_RANK

LANES = 128
V7X_VMEM_BYTES = 64 * 1024 * 1024
VMEM_HEADROOM_BYTES = 6 * 1024 * 1024

ROW_TILE = 1024
NORM_ROWS = 256


def _cparams(semantics, vmem_bytes):
    limit = min(int(vmem_bytes) + VMEM_HEADROOM_BYTES, V7X_VMEM_BYTES - 2 * 1024 * 1024)
    return pltpu.CompilerParams(dimension_semantics=semantics, vmem_limit_bytes=limit)


def _rms(x, gain):
    ms = jnp.mean(x * x, axis=-1, keepdims=True)
    return x * lax.rsqrt(ms + EPS) * gain


def _dot(a, b):
    return jnp.dot(a, b, preferred_element_type=F32)


def _dot_nt(a, b):
    return lax.dot_general(a, b, (((1,), (1,)), ((), ())), preferred_element_type=F32)


def _dot_tn(a, b):
    return lax.dot_general(a, b, (((0,), (0,)), ((), ())), preferred_element_type=F32)


def _split_bf16(x):
    hi = x.astype(BF16)
    lo = (x - hi.astype(F32)).astype(BF16)
    return hi, lo


def _rmsnorm_body(x_ref, g_ref, o_ref):
    o_ref[...] = _rms(x_ref[...], g_ref[...]).astype(o_ref.dtype)


def _rmsnorm(x, gain, out_dtype):
    t, d = x.shape
    tr = NORM_ROWS
    vmem = 2 * tr * d * (4 + jnp.dtype(out_dtype).itemsize) + 4 * tr * d
    return pl.pallas_call(
        _rmsnorm_body,
        grid=(t // tr,),
        in_specs=[pl.BlockSpec((tr, d), lambda i: (i, 0)),
                  pl.BlockSpec((1, d), lambda i: (0, 0))],
        out_specs=pl.BlockSpec((tr, d), lambda i: (i, 0)),
        out_shape=jax.ShapeDtypeStruct((t, d), out_dtype),
        compiler_params=_cparams(("parallel",), vmem),
        name="rmsnorm",
    )(x, gain.reshape(1, d))


def _residual_norm2_body(x_ref, y_ref, g1_ref, g2_ref, x1_ref, h_ref):
    x1 = x_ref[...] + _rms(y_ref[...], g1_ref[...])
    x1_ref[...] = x1
    h_ref[...] = _rms(x1, g2_ref[...]).astype(h_ref.dtype)


def _residual_norm2(x, y, g1, g2):
    t, d = x.shape
    tr = NORM_ROWS
    vmem = 2 * tr * d * (4 + 4 + 4 + 2) + 8 * tr * d
    row = pl.BlockSpec((tr, d), lambda i: (i, 0))
    vec = pl.BlockSpec((1, d), lambda i: (0, 0))
    return pl.pallas_call(
        _residual_norm2_body,
        grid=(t // tr,),
        in_specs=[row, row, vec, vec],
        out_specs=[row, row],
        out_shape=[jax.ShapeDtypeStruct((t, d), F32), jax.ShapeDtypeStruct((t, d), BF16)],
        compiler_params=_cparams(("parallel",), vmem),
        name="residual_norm2",
    )(x, y, g1.reshape(1, d), g2.reshape(1, d))


def _residual_norm_body(x_ref, y_ref, g_ref, o_ref):
    o_ref[...] = x_ref[...] + _rms(y_ref[...], g_ref[...])


def _residual_norm(x, y, gain):
    t, d = x.shape
    tr = NORM_ROWS
    vmem = 2 * tr * d * 12 + 8 * tr * d
    row = pl.BlockSpec((tr, d), lambda i: (i, 0))
    vec = pl.BlockSpec((1, d), lambda i: (0, 0))
    return pl.pallas_call(
        _residual_norm_body,
        grid=(t // tr,),
        in_specs=[row, row, vec],
        out_specs=row,
        out_shape=jax.ShapeDtypeStruct((t, d), F32),
        compiler_params=_cparams(("parallel",), vmem),
        name="residual_norm",
    )(x, y, gain.reshape(1, d))


def _mm_body(a_ref, w_ref, o_ref):
    o_ref[...] = _dot(a_ref[...].astype(BF16), w_ref[...].astype(BF16)).astype(o_ref.dtype)


def _matmul(a, w, *, col0, n_out, tn, out_dtype):
    m, k = a.shape
    tm = min(ROW_TILE, m)
    assert m % tm == 0 and n_out % tn == 0 and col0 % tn == 0 and tn % LANES == 0
    cb0 = col0 // tn
    a_bytes = jnp.dtype(a.dtype).itemsize
    o_bytes = jnp.dtype(out_dtype).itemsize
    vmem = tm * k * a_bytes + 2 * k * tn * 4 + k * tn * 2 + 2 * tm * tn * o_bytes + tm * tn * 4
    return pl.pallas_call(
        _mm_body,
        grid=(m // tm, n_out // tn),
        in_specs=[pl.BlockSpec((tm, k), lambda i, j: (i, 0), pipeline_mode=pl.Buffered(1)),
                  pl.BlockSpec((k, tn), lambda i, j: (0, j + cb0))],
        out_specs=pl.BlockSpec((tm, tn), lambda i, j: (i, j)),
        out_shape=jax.ShapeDtypeStruct((m, n_out), out_dtype),
        compiler_params=_cparams(("parallel", "arbitrary"), vmem),
        name="proj",
    )(a, w)


def _mm_nt_body(a_ref, wt_ref, o_ref):
    o_ref[...] = _dot_nt(a_ref[...].astype(BF16), wt_ref[...].astype(BF16)).astype(o_ref.dtype)


def _matmul_nt(a, wt, *, row0, n_out, tn, out_dtype):
    m, k = a.shape
    tm = min(ROW_TILE, m)
    assert m % tm == 0 and n_out % tn == 0 and tn % LANES == 0 and row0 % 8 == 0
    assert wt.shape[1] == k and row0 + n_out <= wt.shape[0]
    a_bytes = jnp.dtype(a.dtype).itemsize
    o_bytes = jnp.dtype(out_dtype).itemsize
    vmem = tm * k * a_bytes + 2 * k * tn * 4 + k * tn * 2 + 2 * tm * tn * o_bytes + tm * tn * 4
    return pl.pallas_call(
        _mm_nt_body,
        grid=(m // tm, n_out // tn),
        in_specs=[pl.BlockSpec((tm, k), lambda i, j: (i, 0), pipeline_mode=pl.Buffered(1)),
                  pl.BlockSpec((pl.Element(tn), pl.Element(k)),
                               lambda i, j: (pl.multiple_of(row0 + j * tn, 8), 0))],
        out_specs=pl.BlockSpec((tm, tn), lambda i, j: (i, j)),
        out_shape=jax.ShapeDtypeStruct((m, n_out), out_dtype),
        compiler_params=_cparams(("parallel", "arbitrary"), vmem),
        name="proj_in",
    )(a, wt)


def _mm2_body(a1_ref, a2_ref, w_ref, o_ref):
    k1 = a1_ref.shape[1]
    w = w_ref[...].astype(BF16)
    o_ref[...] = (_dot(a1_ref[...].astype(BF16), w[:k1]) + _dot(a2_ref[...].astype(BF16), w[k1:]))


def _matmul2(a1, a2, w, *, tn):
    m, k1 = a1.shape
    k2 = a2.shape[1]
    k, n = w.shape
    assert k == k1 + k2 and n % tn == 0
    tm = min(ROW_TILE, m)
    vmem = (tm * k1 * jnp.dtype(a1.dtype).itemsize + tm * k2 * jnp.dtype(a2.dtype).itemsize
            + 2 * k * tn * 4 + k * tn * 2 + 3 * tm * tn * 4)
    return pl.pallas_call(
        _mm2_body,
        grid=(m // tm, n // tn),
        in_specs=[pl.BlockSpec((tm, k1), lambda i, j: (i, 0), pipeline_mode=pl.Buffered(1)),
                  pl.BlockSpec((tm, k2), lambda i, j: (i, 0), pipeline_mode=pl.Buffered(1)),
                  pl.BlockSpec((k, tn), lambda i, j: (0, j))],
        out_specs=pl.BlockSpec((tm, tn), lambda i, j: (i, j)),
        out_shape=jax.ShapeDtypeStruct((m, n), F32),
        compiler_params=_cparams(("parallel", "arbitrary"), vmem),
        name="out_proj",
    )(a1, a2, w)


def _gate_up_body(a_ref, wg_ref, wu_ref, o_ref):
    a = a_ref[...]
    g = _dot(a, wg_ref[...].astype(BF16))
    u = _dot(a, wu_ref[...].astype(BF16))
    o_ref[...] = (g * (1.0 / (1.0 + jnp.exp(-g))) * u).astype(o_ref.dtype)


def _gate_up(a, w_gate, w_up, *, tn):
    m, k = a.shape
    n = w_gate.shape[1]
    assert n % tn == 0
    tm = min(ROW_TILE, m)
    vmem = tm * k * 2 + 4 * k * tn * 4 + 2 * k * tn * 2 + 2 * tm * tn * 2 + 3 * tm * tn * 4
    wspec = pl.BlockSpec((k, tn), lambda i, j: (0, j))
    return pl.pallas_call(
        _gate_up_body,
        grid=(m // tm, n // tn),
        in_specs=[pl.BlockSpec((tm, k), lambda i, j: (i, 0), pipeline_mode=pl.Buffered(1)), wspec, wspec],
        out_specs=pl.BlockSpec((tm, tn), lambda i, j: (i, j)),
        out_shape=jax.ShapeDtypeStruct((m, n), BF16),
        compiler_params=_cparams(("parallel", "arbitrary"), vmem),
        name="ffn_gate_up",
    )(a, w_gate, w_up)


def _log_sigmoid(x):
    return -(jnp.maximum(-x, 0.0) + jnp.log1p(jnp.exp(-jnp.abs(x))))


def _gla_block(q, k, v, la, chunk_state, n_chunks):
    cp = GLA_CHUNK
    r = n_chunks * cp
    shift = cp.bit_length() - 1
    ri = lax.broadcasted_iota(jnp.int32, (r, r), 0)
    ci = lax.broadcasted_iota(jnp.int32, (r, r), 1)
    tril_mask = jnp.logical_and(jnp.right_shift(ri, shift) == jnp.right_shift(ci, shift), ri >= ci)
    tril_b = jnp.where(tril_mask, 1.0, 0.0).astype(BF16)
    la_hi, la_lo = _split_bf16(la)
    b = _dot(tril_b, la_hi) + _dot(tril_b, la_lo)
    b_last = [b[(c + 1) * cp - 1:(c + 1) * cp, :] for c in range(n_chunks)]
    b_end = jnp.concatenate([jnp.broadcast_to(bl, (cp, GLA_DK)) for bl in b_last], axis=0)
    q_dec = (q * (GLA_DK ** -0.5) * jnp.exp(b)).astype(BF16)
    k_inv = (k * jnp.exp(-b)).astype(BF16)
    k_carry = k * jnp.exp(b_end - b)
    scores = jnp.where(tril_mask, _dot_nt(q_dec, k_inv), 0.0).astype(BF16)
    v_b = v.astype(BF16)
    k_carry_b = k_carry.astype(BF16)
    o_intra = _dot(scores, v_b)
    rows16 = lax.broadcasted_iota(jnp.int32, (16, GLA_DK), 0)
    ones16 = jnp.ones((16, GLA_DV), BF16)
    outs = []
    for c in range(n_chunks):
        rows = slice(c * cp, (c + 1) * cp)
        load, store = chunk_state(c)
        s = load()
        outs.append(o_intra[rows] + _dot(q_dec[rows], s.astype(BF16)))
        decay = jnp.exp(b_last[c])
        d_hi = decay.astype(BF16).astype(F32)
        d_mid = (decay - d_hi).astype(BF16).astype(F32)
        d_lo = (decay - d_hi - d_mid).astype(BF16).astype(F32)
        split = jnp.where(rows16 == 0, d_hi, jnp.where(rows16 == 1, d_mid, jnp.where(rows16 == 2, d_lo, 0.0)))
        store(s * _dot_tn(split.astype(BF16), ones16) + _dot_tn(k_carry_b[rows], v_b[rows]))
    return jnp.concatenate(outs, axis=0)


def _gla_body(*refs, c_valid, n_chunks, n_seqs, has_state):
    if has_state:
        (q_ref, k_ref, v_ref, g_ref, ga_ref, wa_ref, ba_ref, gn_ref, s0_ref,
         o_ref, sout_ref, s_scr) = refs
    else:
        (q_ref, k_ref, v_ref, g_ref, ga_ref, wa_ref, ba_ref, gn_ref,
         o_ref, sout_ref, s_scr) = refs
        s0_ref = None
    blk = pl.program_id(2)
    cp = GLA_CHUNK
    total = n_seqs * n_chunks
    group = min(total, 4)
    assert total % group == 0 and (n_chunks % group == 0 or group % n_chunks == 0)

    @pl.when(blk == 0)
    def _():
        if has_state:
            s_scr[...] = s0_ref[:, 0]
        else:
            s_scr[...] = jnp.zeros(s_scr.shape, F32)

    wa_hi, wa_lo = _split_bf16(wa_ref[...])
    ga_hi, ga_lo = _split_bf16(ga_ref[...])
    pre = _dot(ga_hi, wa_hi) + _dot(ga_hi, wa_lo) + _dot(ga_lo, wa_hi) + ba_ref[...]
    la_all = _log_sigmoid(pre) * (1.0 / GLA_TAU)

    def padded(x, j0):
        if c_valid == cp:
            return x[j0 * cp:(j0 + group) * cp]
        zero = jnp.zeros((cp - c_valid, x.shape[1]), F32)
        parts = []
        for j in range(j0, j0 + group):
            parts += [x[j * c_valid:(j + 1) * c_valid], zero]
        return jnp.concatenate(parts, axis=0)

    def chunk_state(j0):
        def access(c):
            sq = (j0 + c) // n_chunks

            def store(val):
                s_scr[sq] = val
            return (lambda: s_scr[sq]), store
        return access

    q_all, k_all, v_all = q_ref[...], k_ref[...], v_ref[...]
    outs = []
    for j0 in range(0, total, group):
        o = _gla_block(padded(q_all, j0), padded(k_all, j0), padded(v_all, j0), padded(la_all, j0),
                       chunk_state(j0), group)
        outs += [o[c * cp:c * cp + c_valid] for c in range(group)] if c_valid != cp else [o]
    o = outs[0] if len(outs) == 1 else jnp.concatenate(outs, axis=0)
    gate = g_ref[...]
    o_ref[...] = (_rms(o, gn_ref[...]) * (gate * (1.0 / (1.0 + jnp.exp(-gate))))).astype(o_ref.dtype)

    @pl.when(blk == pl.num_programs(2) - 1)
    def _():
        sout_ref[:, 0] = s_scr[...]


def _gla(zg, zga, w_a2p, b_a, gla_norm, state0, *, batch, seq, c_valid, n_chunks, n_seqs):
    t = zg.shape[0]
    tb = n_seqs * n_chunks * c_valid
    assert (batch * seq) == t and (n_seqs * seq) % tb == 0
    nb = (n_seqs * seq) // tb
    bg = batch // n_seqs
    has_state = state0 is not None

    def rows(col):
        return lambda b, h, j: (b * nb + j, col(h))

    in_specs = [
        pl.BlockSpec((tb, GLA_DK), rows(lambda h: h)),
        pl.BlockSpec((tb, GLA_DK), rows(lambda h: GLA_HEADS + h)),
        pl.BlockSpec((tb, GLA_DV), rows(lambda h: GLA_HEADS + h)),
        pl.BlockSpec((tb, GLA_DV), rows(lambda h: 2 * GLA_HEADS + h)),
        pl.BlockSpec((tb, LANES), rows(lambda h: 0)),
        pl.BlockSpec((LANES, GLA_DK), lambda b, h, j: (0, h)),
        pl.BlockSpec((1, GLA_DK), lambda b, h, j: (0, h)),
        pl.BlockSpec((1, GLA_DV), lambda b, h, j: (0, 0)),
    ]
    args = [zg, zg, zg, zg, zga, w_a2p, b_a.reshape(1, -1), gla_norm.reshape(1, -1)]
    state_spec = pl.BlockSpec((n_seqs, 1, GLA_DK, GLA_DV), lambda b, h, j: (b, h, 0, 0))
    if has_state:
        in_specs.append(state_spec)
        args.append(state0)
    state_bytes = n_seqs * GLA_DK * GLA_DV * 4
    vmem = (2 * tb * (2 * GLA_DK + 2 * GLA_DV + LANES) * 4 + 2 * tb * GLA_DV * 2
            + (5 if has_state else 3) * state_bytes + 16 * 1024 * 1024)
    body = functools.partial(_gla_body, c_valid=c_valid, n_chunks=n_chunks, n_seqs=n_seqs,
                             has_state=has_state)
    return pl.pallas_call(
        body,
        grid=(bg, GLA_HEADS, nb),
        in_specs=in_specs,
        out_specs=[pl.BlockSpec((tb, GLA_DV), rows(lambda h: h)), state_spec],
        out_shape=[jax.ShapeDtypeStruct((t, GLA_WIDTH), BF16),
                   jax.ShapeDtypeStruct((batch, GLA_HEADS, GLA_DK, GLA_DV), F32)],
        scratch_shapes=[pltpu.VMEM((n_seqs, GLA_DK, GLA_DV), F32)],
        compiler_params=_cparams(("parallel", "parallel", "arbitrary"), vmem),
        name="gla",
    )(*args)


def _lambda(lq1_ref, lk1_ref, lq2_ref, lk2_ref, lam_init):
    a = jnp.sum(lq1_ref[...] * lk1_ref[...], axis=-1, keepdims=True)
    b = jnp.sum(lq2_ref[...] * lk2_ref[...], axis=-1, keepdims=True)
    return jnp.exp(a) - jnp.exp(b) + lam_init


def _lane_tile(x, n):
    return x if n == 1 else jnp.concatenate([x] * n, axis=1)


SOFTMAX_ROWS = 64


def _diff_prompt_body(q_ref, k_ref, v_ref, lq1_ref, lk1_ref, lq2_ref, lk2_ref, dn_ref, o_ref,
                      s_scr, p_scr, corr_scr, m_scr, l_scr, acc_scr, *, tq, tk, lam_init):
    qi = pl.program_id(2)
    dh = DIFF_HEAD_DIM
    rs = SOFTMAX_ROWS
    q = q_ref[...] * (dh ** -0.5)
    q_maps = (q[:, :dh].astype(BF16), q[:, dh:].astype(BF16))
    m_scr[...] = jnp.full(m_scr.shape, -jnp.inf, F32)
    l_scr[...] = jnp.zeros(l_scr.shape, F32)
    acc_scr[...] = jnp.zeros(acc_scr.shape, F32)
    n_kv = (qi * tq + tq + tk - 1) // tk
    row = lax.broadcasted_iota(jnp.int32, (rs, tk), 0)
    kcol = lax.broadcasted_iota(jnp.int32, (rs, tk), 1)

    def step(j, masked):
        k0 = pl.multiple_of(j * tk, tk)
        kb = k_ref[pl.ds(k0, tk), :]
        v_b = v_ref[pl.ds(k0, tk), :].astype(BF16)
        for c in range(2):
            s_scr[c] = _dot_nt(q_maps[c], kb[:, c * dh:(c + 1) * dh].astype(BF16))
        for c in range(2):
            for r0 in range(0, tq, rs):
                rows = slice(r0, r0 + rs)
                s = s_scr[c, rows, :]
                if masked:
                    s = jnp.where(kcol + j * tk <= row + (qi * tq + r0), s, -jnp.inf)
                m_old = m_scr[c, rows, :]
                m_new = jnp.maximum(m_old, jnp.max(s, axis=-1, keepdims=True))
                corr = jnp.exp(m_old - m_new)
                p = jnp.exp(s - _lane_tile(m_new, tk // LANES))
                l_scr[c, rows, :] = l_scr[c, rows, :] * corr + jnp.sum(p, axis=-1, keepdims=True)
                m_scr[c, rows, :] = m_new
                corr_scr[c, rows, :] = corr
                p_scr[c, rows, :] = p.astype(BF16)
        for c in range(2):
            acc_scr[c] = acc_scr[c] * _lane_tile(corr_scr[c], DIFF_VDIM // LANES) + _dot(p_scr[c], v_b)

    def full_step(j, carry):
        step(j, False)
        return carry

    lax.fori_loop(0, n_kv - 1, full_step, 0)
    step(n_kv - 1, True)
    lam = _lambda(lq1_ref, lk1_ref, lq2_ref, lk2_ref, lam_init)
    o = acc_scr[0] / l_scr[0][:, :1] - lam * (acc_scr[1] / l_scr[1][:, :1])
    o_ref[...] = (_rms(o, dn_ref[...]) * (1.0 - lam_init)).astype(o_ref.dtype)


def _diff_prompt(zq, zk, zv, lams, diff_norm, *, batch, seq, lam_init, tq=512, tk=512):
    t = zq.shape[0]
    nq = seq // tq
    hw = 2 * DIFF_HEAD_DIM
    vec = pl.BlockSpec((1, DIFF_HEAD_DIM), lambda b, h, i: (0, 0))
    kv_spec = pl.BlockSpec((seq, hw), lambda b, h, i: (b, h))
    vmem = 2 * tq * hw * 4 + 4 * seq * hw * 4 + 2 * tq * hw * 2 + 4 * tq * LANES * 4 + 2 * tq * hw * 4 \
        + 8 * tq * tk * 4
    body = functools.partial(_diff_prompt_body, tq=tq, tk=tk, lam_init=lam_init)
    return pl.pallas_call(
        body,
        grid=(batch, DIFF_HEADS, nq),
        in_specs=[pl.BlockSpec((tq, hw), lambda b, h, i: (b * nq + i, h)), kv_spec, kv_spec,
                  vec, vec, vec, vec, pl.BlockSpec((1, DIFF_VDIM), lambda b, h, i: (0, 0))],
        out_specs=pl.BlockSpec((tq, DIFF_VDIM), lambda b, h, i: (b * nq + i, h)),
        out_shape=jax.ShapeDtypeStruct((t, DIFF_WIDTH), BF16),
        scratch_shapes=[pltpu.VMEM((2, tq, tk), F32), pltpu.VMEM((2, tq, tk), BF16),
                        pltpu.VMEM((2, tq, LANES), F32), pltpu.VMEM((2, tq, LANES), F32),
                        pltpu.VMEM((2, tq, LANES), F32), pltpu.VMEM((2, tq, DIFF_VDIM), F32)],
        compiler_params=_cparams(("parallel", "parallel", "arbitrary"), vmem),
        name="diff_attn_prompt",
    )(zq, zk, zv, *lams, diff_norm.reshape(1, -1))


def _diff_sample_body(pt_ref, q_ref, kn_ref, vn_ref, *rest, n_tok, pages_per_step, lam_init):
    kp_refs = rest[:pages_per_step]
    vp_refs = rest[pages_per_step:2 * pages_per_step]
    (lq1_ref, lk1_ref, lq2_ref, lk2_ref, dn_ref, o_ref,
     q_scr, m_scr, l_scr, acc_scr) = rest[2 * pages_per_step:]
    del pt_ref
    step = pl.program_id(1)
    dh = DIFF_HEAD_DIM
    hw = 2 * dh
    n_half = DIFF_HEADS * n_tok
    page_rows = PAGE_SIZE * DIFF_HEADS
    tok_bits = n_tok.bit_length() - 1

    def row_head(r):
        return jnp.bitwise_and(jnp.right_shift(r, tok_bits), DIFF_HEADS - 1)

    @pl.when(step == 0)
    def _():
        q = q_ref[...] * (dh ** -0.5)
        kn = kn_ref[...]
        vn = vn_ref[...]
        zpad_k = jnp.zeros((PAGE_SIZE - n_half, dh), F32)
        s_parts = []
        for c in range(2):
            qc = jnp.concatenate([q[:, h * hw + c * dh:h * hw + (c + 1) * dh] for h in range(DIFF_HEADS)],
                                 axis=0).astype(BF16)
            q_scr[c] = qc
            knc = jnp.concatenate([kn[:, h * hw + c * dh:h * hw + (c + 1) * dh] for h in range(DIFF_HEADS)]
                                  + [zpad_k], axis=0).astype(BF16)
            s_parts.append(_dot_nt(qc, knc))
        s = jnp.concatenate(s_parts, axis=0)
        r = lax.broadcasted_iota(jnp.int32, s.shape, 0)
        col = lax.broadcasted_iota(jnp.int32, s.shape, 1)
        visible = jnp.logical_and(jnp.right_shift(col, tok_bits) == row_head(r),
                                  jnp.bitwise_and(col, n_tok - 1) <= jnp.bitwise_and(r, n_tok - 1))
        s = jnp.where(visible, s, -jnp.inf)
        m = jnp.max(s, axis=-1, keepdims=True)
        p = jnp.exp(s - m)
        m_scr[...] = jnp.broadcast_to(m, m_scr.shape)
        l_scr[...] = jnp.broadcast_to(jnp.sum(p, axis=-1, keepdims=True), l_scr.shape)
        vn_rows = jnp.concatenate([vn[:, h * hw:(h + 1) * hw] for h in range(DIFF_HEADS)]
                                  + [jnp.zeros((PAGE_SIZE - n_half, hw), F32)], axis=0).astype(BF16)
        acc_scr[...] = _dot(p.astype(BF16), vn_rows)

    q0 = q_scr[0]
    q1 = q_scr[1]

    def page_scores(kp_ref):
        k0 = kp_ref[0, pl.ds(0, page_rows, stride=2), :].astype(BF16)
        k1 = kp_ref[0, pl.ds(1, page_rows, stride=2), :].astype(BF16)
        return jnp.concatenate([_dot_nt(q0, k0), _dot_nt(q1, k1)], axis=0)

    r = lax.broadcasted_iota(jnp.int32, (2 * n_half, page_rows), 0)
    col = lax.broadcasted_iota(jnp.int32, (2 * n_half, page_rows), 1)
    own_head = jnp.bitwise_and(col, DIFF_HEADS - 1) == row_head(r)
    m, l, acc = m_scr[...], l_scr[...], acc_scr[...]
    s_next = page_scores(kp_refs[0])
    for i in range(pages_per_step):
        s = jnp.where(own_head, s_next, -jnp.inf)
        if i + 1 < pages_per_step:
            s_next = page_scores(kp_refs[i + 1])
        m_new = jnp.maximum(m, jnp.max(s, axis=-1, keepdims=True))
        corr = jnp.exp(m - m_new)
        p = jnp.exp(s - _lane_tile(m_new, page_rows // LANES))
        l = l * corr + jnp.sum(p, axis=-1, keepdims=True)
        acc = acc * _lane_tile(corr, DIFF_VDIM // LANES) + _dot(p.astype(BF16), vp_refs[i][0].astype(BF16))
        m = m_new
    m_scr[...] = m
    l_scr[...] = l
    acc_scr[...] = acc

    @pl.when(step == pl.num_programs(1) - 1)
    def _():
        lam = _lambda(lq1_ref, lk1_ref, lq2_ref, lk2_ref, lam_init)
        o_all = acc_scr[...] / l_scr[...][:, :1]
        dn = dn_ref[...]
        outs = []
        for h in range(DIFF_HEADS):
            o = o_all[h * n_tok:(h + 1) * n_tok] - lam * o_all[n_half + h * n_tok:n_half + (h + 1) * n_tok]
            outs.append(_rms(o, dn) * (1.0 - lam_init))
        o_ref[...] = jnp.concatenate(outs, axis=1).astype(o_ref.dtype)


def _diff_sample(zq, zk, zv, cache_k, cache_v, page_table, lams, diff_norm, *, n_tok, lam_init,
                 pages_per_step=4):
    t = zq.shape[0]
    batch, n_pages = page_table.shape
    assert n_pages % pages_per_step == 0 and t == batch * n_tok
    assert n_tok & (n_tok - 1) == 0 and n_tok % 8 == 0 and DIFF_HEADS * n_tok <= PAGE_SIZE
    n_steps = n_pages // pages_per_step
    dh = DIFF_HEAD_DIM
    hw = 2 * dh
    n_half = DIFF_HEADS * n_tok
    n_rows = 2 * n_half
    page_rows = PAGE_SIZE * DIFF_HEADS
    tok_spec = pl.BlockSpec((n_tok, DIFF_WIDTH), lambda b, s, pt: (b, 0))

    def page_spec(shape, i):
        return pl.BlockSpec((1,) + shape,
                            lambda b, s, pt: (pt[b * n_pages + s * pages_per_step + i], 0, 0))

    vec = pl.BlockSpec((1, dh), lambda b, s, pt: (0, 0))
    in_specs = ([tok_spec, tok_spec, tok_spec]
                + [page_spec((2 * page_rows, dh), i) for i in range(pages_per_step)]
                + [page_spec((page_rows, hw), i) for i in range(pages_per_step)]
                + [vec, vec, vec, vec, pl.BlockSpec((1, DIFF_VDIM), lambda b, s, pt: (0, 0))])
    page_bytes = page_rows * hw * 4
    vmem = 4 * pages_per_step * page_bytes + 2 * pages_per_step * page_bytes \
        + 6 * n_rows * pages_per_step * page_rows * 4 + 4 * 1024 * 1024
    body = functools.partial(_diff_sample_body, n_tok=n_tok, pages_per_step=pages_per_step,
                             lam_init=lam_init)
    grid_spec = pltpu.PrefetchScalarGridSpec(
        num_scalar_prefetch=1,
        grid=(batch, n_steps),
        in_specs=in_specs,
        out_specs=pl.BlockSpec((n_tok, DIFF_WIDTH), lambda b, s, pt: (b, 0)),
        scratch_shapes=[pltpu.VMEM((2, n_half, dh), BF16), pltpu.VMEM((n_rows, LANES), F32),
                        pltpu.VMEM((n_rows, LANES), F32), pltpu.VMEM((n_rows, DIFF_VDIM), F32)],
    )
    return pl.pallas_call(
        body,
        grid_spec=grid_spec,
        out_shape=jax.ShapeDtypeStruct((t, DIFF_WIDTH), F32),
        compiler_params=_cparams(("parallel", "arbitrary"), vmem),
        name="diff_attn_sample",
    )(page_table.reshape(-1), zq, zk, zv, *([cache_k] * pages_per_step), *([cache_v] * pages_per_step),
      *lams, diff_norm.reshape(1, -1))


def _layer(x, p, lam_init, *, batch, seq, gla_state, attend):
    h = _rmsnorm(x, p['pre_mix_norm'], BF16)
    wt = p['w_in_t']
    zg = _matmul_nt(h, wt, row0=0, n_out=GLA_COLS, tn=512, out_dtype=F32)
    zga = _matmul_nt(h, wt, row0=GATE_COL0, n_out=LANES, tn=LANES, out_dtype=F32)
    zq = _matmul_nt(h, wt, row0=DIFF_COL0, n_out=DIFF_WIDTH, tn=512, out_dtype=F32)
    zk = _matmul_nt(h, wt, row0=DIFF_COL0 + DIFF_WIDTH, n_out=DIFF_WIDTH, tn=512, out_dtype=F32)
    zv = _matmul_nt(h, wt, row0=DIFF_COL0 + 2 * DIFF_WIDTH, n_out=DIFF_WIDTH, tn=512, out_dtype=F32)
    if gla_state is None:
        o_gla, s_out = _gla(zg, zga, p['w_a2p'], p['b_a'], p['gla_norm'], None, batch=batch, seq=seq,
                            c_valid=GLA_CHUNK, n_chunks=8, n_seqs=1)
    else:
        o_gla, s_out = _gla(zg, zga, p['w_a2p'], p['b_a'], p['gla_norm'], gla_state, batch=batch,
                            seq=seq, c_valid=seq, n_chunks=1, n_seqs=4)
    o_diff = attend(zq, zk, zv)
    mixed = _matmul2(o_gla, o_diff, p['w_o'], tn=512)
    x1, h2 = _residual_norm2(x, mixed, p['post_mix_norm'], p['pre_ffn_norm'])
    act = _gate_up(h2, p['w_gate'], p['w_up'], tn=256)
    f = _matmul(act, p['w_down'], col0=0, n_out=D_MODEL, tn=256, out_dtype=F32)
    y = _residual_norm(x1, f, p['post_ffn_norm'])
    return y, zk, zv, s_out


def kernel(x_prompt, x_sample, cache_k, cache_v, state_gla, page_table, pre_mix_norm, w_in, w_a2, b_a, gla_norm, lambda_q1, lambda_k1, lambda_q2, lambda_k2, diff_norm, w_o, post_mix_norm, pre_ffn_norm, w_gate, w_up, w_down, post_ffn_norm):
    depth = w_in.shape[0]
    bp, sp, d = x_prompt.shape
    bs, ss, _ = x_sample.shape
    n_pool = cache_k.shape[1]
    xp = x_prompt.reshape(bp * sp, d)
    xs = x_sample.reshape(bs * ss, d)
    outs = [[] for _ in range(6)]
    for l in range(depth):
        lam_init = 0.8 - 0.6 * math.exp(-0.3 * l)
        p = {
            'pre_mix_norm': pre_mix_norm[l], 'w_in_t': jnp.swapaxes(w_in[l], 0, 1),
            'w_a2p': jnp.pad(w_a2[l], ((0, LANES - GLA_GATE_RANK), (0, 0))),
            'b_a': b_a[l], 'gla_norm': gla_norm[l], 'w_o': w_o[l],
            'post_mix_norm': post_mix_norm[l], 'pre_ffn_norm': pre_ffn_norm[l],
            'w_gate': w_gate[l], 'w_up': w_up[l], 'w_down': w_down[l],
            'post_ffn_norm': post_ffn_norm[l],
        }
        lams = [v[l].reshape(1, -1) for v in (lambda_q1, lambda_k1, lambda_q2, lambda_k2)]
        ck = cache_k[l].reshape(n_pool, 2 * PAGE_SIZE * DIFF_HEADS, DIFF_HEAD_DIM)
        cv = cache_v[l].reshape(n_pool, PAGE_SIZE * DIFF_HEADS, DIFF_VDIM)
        attend_p = functools.partial(_diff_prompt, lams=lams, diff_norm=diff_norm[l], batch=bp, seq=sp,
                                     lam_init=lam_init)
        attend_s = functools.partial(_diff_sample, cache_k=ck, cache_v=cv, page_table=page_table,
                                     lams=lams, diff_norm=diff_norm[l], n_tok=ss, lam_init=lam_init)
        xp, kp, vp, s_p = _layer(xp, p, lam_init, batch=bp, seq=sp, gla_state=None, attend=attend_p)
        xs, ks, vs, s_s = _layer(xs, p, lam_init, batch=bs, seq=ss, gla_state=state_gla[l],
                                 attend=attend_s)
        outs[0].append(kp.reshape(bp, sp, DIFF_HEADS, 2, DIFF_HEAD_DIM))
        outs[1].append(vp.reshape(bp, sp, DIFF_HEADS, DIFF_VDIM))
        outs[2].append(s_p)
        outs[3].append(ks.reshape(bs, ss, DIFF_HEADS, 2, DIFF_HEAD_DIM))
        outs[4].append(vs.reshape(bs, ss, DIFF_HEADS, DIFF_VDIM))
        outs[5].append(s_s)
    k_p, v_p, st_p, k_s, v_s, st_s = [o[0][None] if depth == 1 else jnp.stack(o) for o in outs]
    return (xp.reshape(bp, sp, d), xs.reshape(bs, ss, d), k_p, v_p, st_p, k_s, v_s, st_s)
```

```python
import functools
import math

import jax
import jax.numpy as jnp
from jax import lax
from jax.experimental import pallas as pl
from jax.experimental.pallas import tpu as pltpu

F32 = jnp.float32
BF16 = jnp.bfloat16

D_MODEL = 4096
GLA_HEADS = 4
GLA_DK = 256
GLA_DV = 512
GLA_KEY_WIDTH = GLA_HEADS * GLA_DK
GLA_WIDTH = GLA_HEADS * GLA_DV
GLA_GATE_RANK = 16
GLA_TAU = 16.0
GLA_CHUNK = 64
DIFF_HEADS = 8
DIFF_HEAD_DIM = 128
DIFF_VDIM = 256
DIFF_WIDTH = DIFF_HEADS * DIFF_VDIM
PAGE_SIZE = 128
EPS = 1e-6

GLA_COLS = 2 * GLA_KEY_WIDTH + 2 * GLA_WIDTH
GATE_COL0 = GLA_COLS
DIFF_COL0 = GLA_COLS + GLA_GATE_RANK

LANES = 128
V7X_VMEM_BYTES = 64 * 1024 * 1024
VMEM_HEADROOM_BYTES = 6 * 1024 * 1024

ROW_TILE = 1024
GATE_UP_ROW_TILE = 2048
NORM_ROWS = 256


def _cparams(semantics, vmem_bytes):
    limit = min(int(vmem_bytes) + VMEM_HEADROOM_BYTES, V7X_VMEM_BYTES - 2 * 1024 * 1024)
    return pltpu.CompilerParams(dimension_semantics=semantics, vmem_limit_bytes=limit)


def _rms(x, gain):
    ms = jnp.mean(x * x, axis=-1, keepdims=True)
    return x * lax.rsqrt(ms + EPS) * gain


def _dot(a, b):
    return jnp.dot(a, b, preferred_element_type=F32)


def _dot_nt(a, b):
    return lax.dot_general(a, b, (((1,), (1,)), ((), ())), preferred_element_type=F32)


def _dot_tn(a, b):
    return lax.dot_general(a, b, (((0,), (0,)), ((), ())), preferred_element_type=F32)


def _split_bf16(x):
    hi = x.astype(BF16)
    lo = (x - hi.astype(F32)).astype(BF16)
    return hi, lo


def _rmsnorm_body(x_ref, g_ref, o_ref):
    o_ref[...] = _rms(x_ref[...], g_ref[...]).astype(o_ref.dtype)


def _rmsnorm(x, gain, out_dtype):
    t, d = x.shape
    tr = NORM_ROWS
    vmem = 2 * tr * d * (4 + jnp.dtype(out_dtype).itemsize) + 4 * tr * d
    return pl.pallas_call(
        _rmsnorm_body,
        grid=(t // tr,),
        in_specs=[pl.BlockSpec((tr, d), lambda i: (i, 0)),
                  pl.BlockSpec((1, d), lambda i: (0, 0))],
        out_specs=pl.BlockSpec((tr, d), lambda i: (i, 0)),
        out_shape=jax.ShapeDtypeStruct((t, d), out_dtype),
        compiler_params=_cparams(("parallel",), vmem),
        name="rmsnorm",
    )(x, gain.reshape(1, d))


def _residual_norm2_body(x_ref, y_ref, g1_ref, g2_ref, x1_ref, h_ref):
    x1 = x_ref[...] + _rms(y_ref[...], g1_ref[...])
    x1_ref[...] = x1
    h_ref[...] = _rms(x1, g2_ref[...]).astype(h_ref.dtype)


def _residual_norm2(x, y, g1, g2):
    t, d = x.shape
    tr = NORM_ROWS
    vmem = 2 * tr * d * (4 + 4 + 4 + 2) + 8 * tr * d
    row = pl.BlockSpec((tr, d), lambda i: (i, 0))
    vec = pl.BlockSpec((1, d), lambda i: (0, 0))
    return pl.pallas_call(
        _residual_norm2_body,
        grid=(t // tr,),
        in_specs=[row, row, vec, vec],
        out_specs=[row, row],
        out_shape=[jax.ShapeDtypeStruct((t, d), F32), jax.ShapeDtypeStruct((t, d), BF16)],
        compiler_params=_cparams(("parallel",), vmem),
        name="residual_norm2",
    )(x, y, g1.reshape(1, d), g2.reshape(1, d))


def _residual_norm_body(x_ref, y_ref, g_ref, o_ref):
    o_ref[...] = x_ref[...] + _rms(y_ref[...], g_ref[...])


def _residual_norm(x, y, gain):
    t, d = x.shape
    tr = NORM_ROWS
    vmem = 2 * tr * d * 12 + 8 * tr * d
    row = pl.BlockSpec((tr, d), lambda i: (i, 0))
    vec = pl.BlockSpec((1, d), lambda i: (0, 0))
    return pl.pallas_call(
        _residual_norm_body,
        grid=(t // tr,),
        in_specs=[row, row, vec],
        out_specs=row,
        out_shape=jax.ShapeDtypeStruct((t, d), F32),
        compiler_params=_cparams(("parallel",), vmem),
        name="residual_norm",
    )(x, y, gain.reshape(1, d))


def _mm_body(a_ref, w_ref, o_ref):
    o_ref[...] = _dot(a_ref[...].astype(BF16), w_ref[...].astype(BF16)).astype(o_ref.dtype)


def _matmul(a, w, *, col0, n_out, tn, out_dtype):
    m, k = a.shape
    tm = min(ROW_TILE, m)
    assert m % tm == 0 and n_out % tn == 0 and col0 % tn == 0 and tn % LANES == 0
    cb0 = col0 // tn
    a_bytes = jnp.dtype(a.dtype).itemsize
    o_bytes = jnp.dtype(out_dtype).itemsize
    vmem = tm * k * a_bytes + 2 * k * tn * 4 + k * tn * 2 + 2 * tm * tn * o_bytes + tm * tn * 4
    return pl.pallas_call(
        _mm_body,
        grid=(m // tm, n_out // tn),
        in_specs=[pl.BlockSpec((tm, k), lambda i, j: (i, 0), pipeline_mode=pl.Buffered(1)),
                  pl.BlockSpec((k, tn), lambda i, j: (0, j + cb0))],
        out_specs=pl.BlockSpec((tm, tn), lambda i, j: (i, j)),
        out_shape=jax.ShapeDtypeStruct((m, n_out), out_dtype),
        compiler_params=_cparams(("parallel", "arbitrary"), vmem),
        name="proj",
    )(a, w)


def _mm_nt_body(a_ref, wt_ref, o_ref):
    o_ref[...] = _dot_nt(a_ref[...].astype(BF16), wt_ref[...].astype(BF16)).astype(o_ref.dtype)


def _matmul_nt(a, wt, *, row0, n_out, tn, out_dtype):
    m, k = a.shape
    tm = min(ROW_TILE, m)
    assert m % tm == 0 and n_out % tn == 0 and tn % LANES == 0 and row0 % 8 == 0
    assert wt.shape[1] == k and row0 + n_out <= wt.shape[0]
    a_bytes = jnp.dtype(a.dtype).itemsize
    o_bytes = jnp.dtype(out_dtype).itemsize
    vmem = 2 * tm * k * a_bytes + 2 * k * tn * 4 + k * tn * 2 + 2 * tm * tn * o_bytes + tm * tn * 4
    return pl.pallas_call(
        _mm_nt_body,
        grid=(m // tm, n_out // tn),
        in_specs=[pl.BlockSpec((tm, k), lambda i, j: (i, 0)),
                  pl.BlockSpec((pl.Element(tn), pl.Element(k)),
                               lambda i, j: (pl.multiple_of(row0 + j * tn, 8), 0))],
        out_specs=pl.BlockSpec((tm, tn), lambda i, j: (i, j)),
        out_shape=jax.ShapeDtypeStruct((m, n_out), out_dtype),
        compiler_params=_cparams(("parallel", "arbitrary"), vmem),
        name="proj_in",
    )(a, wt)


def _mm2_body(a1_ref, a2_ref, w_ref, o_ref):
    k1 = a1_ref.shape[1]
    w = w_ref[...].astype(BF16)
    o_ref[...] = (_dot(a1_ref[...].astype(BF16), w[:k1]) + _dot(a2_ref[...].astype(BF16), w[k1:]))


def _matmul2(a1, a2, w, *, tn):
    m, k1 = a1.shape
    k2 = a2.shape[1]
    k, n = w.shape
    assert k == k1 + k2 and n % tn == 0
    tm = min(ROW_TILE, m)
    vmem = (2 * tm * k1 * jnp.dtype(a1.dtype).itemsize + 2 * tm * k2 * jnp.dtype(a2.dtype).itemsize
            + 2 * k * tn * 4 + k * tn * 2 + 3 * tm * tn * 4)
    return pl.pallas_call(
        _mm2_body,
        grid=(m // tm, n // tn),
        in_specs=[pl.BlockSpec((tm, k1), lambda i, j: (i, 0)),
                  pl.BlockSpec((tm, k2), lambda i, j: (i, 0)),
                  pl.BlockSpec((k, tn), lambda i, j: (0, j))],
        out_specs=pl.BlockSpec((tm, tn), lambda i, j: (i, j)),
        out_shape=jax.ShapeDtypeStruct((m, n), F32),
        compiler_params=_cparams(("parallel", "arbitrary"), vmem),
        name="out_proj",
    )(a1, a2, w)


def _gate_up_body(a_ref, wg_ref, wu_ref, o_ref):
    a = a_ref[...]
    g = _dot(a, wg_ref[...].astype(BF16))
    u = _dot(a, wu_ref[...].astype(BF16))
    o_ref[...] = (g * (1.0 / (1.0 + jnp.exp(-g))) * u).astype(o_ref.dtype)


def _gate_up(a, w_gate, w_up, *, tn):
    m, k = a.shape
    n = w_gate.shape[1]
    assert n % tn == 0
    tm = min(GATE_UP_ROW_TILE, m)
    vmem = tm * k * 2 + 4 * k * tn * 4 + 2 * k * tn * 2 + 2 * tm * tn * 2 + 3 * tm * tn * 4
    wspec = pl.BlockSpec((k, tn), lambda i, j: (0, j))
    return pl.pallas_call(
        _gate_up_body,
        grid=(m // tm, n // tn),
        in_specs=[pl.BlockSpec((tm, k), lambda i, j: (i, 0), pipeline_mode=pl.Buffered(1)), wspec, wspec],
        out_specs=pl.BlockSpec((tm, tn), lambda i, j: (i, j)),
        out_shape=jax.ShapeDtypeStruct((m, n), BF16),
        compiler_params=_cparams(("parallel", "arbitrary"), vmem),
        name="ffn_gate_up",
    )(a, w_gate, w_up)


def _log_sigmoid(x):
    return -(jnp.maximum(-x, 0.0) + jnp.log1p(jnp.exp(-jnp.abs(x))))


def _gla_block(q, k, v, la, chunk_state, n_chunks):
    cp = GLA_CHUNK
    r = n_chunks * cp
    shift = cp.bit_length() - 1
    ri = lax.broadcasted_iota(jnp.int32, (r, r), 0)
    ci = lax.broadcasted_iota(jnp.int32, (r, r), 1)
    tril_mask = jnp.logical_and(jnp.right_shift(ri, shift) == jnp.right_shift(ci, shift), ri >= ci)
    tril_b = jnp.where(tril_mask, 1.0, 0.0).astype(BF16)
    la_hi, la_lo = _split_bf16(la)
    b = _dot(tril_b, la_hi) + _dot(tril_b, la_lo)
    b_last = [b[(c + 1) * cp - 1:(c + 1) * cp, :] for c in range(n_chunks)]
    b_end = jnp.concatenate([jnp.broadcast_to(bl, (cp, GLA_DK)) for bl in b_last], axis=0)
    q_dec = (q * (GLA_DK ** -0.5) * jnp.exp(b)).astype(BF16)
    k_inv = (k * jnp.exp(-b)).astype(BF16)
    k_carry = k * jnp.exp(b_end - b)
    scores = jnp.where(tril_mask, _dot_nt(q_dec, k_inv), 0.0).astype(BF16)
    v_b = v.astype(BF16)
    k_carry_b = k_carry.astype(BF16)
    o_intra = _dot(scores, v_b)
    rows16 = lax.broadcasted_iota(jnp.int32, (16, GLA_DK), 0)
    ones16 = jnp.ones((16, GLA_DV), BF16)
    outs = []
    for c in range(n_chunks):
        rows = slice(c * cp, (c + 1) * cp)
        load, store = chunk_state(c)
        s = load()
        outs.append(o_intra[rows] + _dot(q_dec[rows], s.astype(BF16)))
        decay = jnp.exp(b_last[c])
        d_hi = decay.astype(BF16).astype(F32)
        d_mid = (decay - d_hi).astype(BF16).astype(F32)
        d_lo = (decay - d_hi - d_mid).astype(BF16).astype(F32)
        split = jnp.where(rows16 == 0, d_hi, jnp.where(rows16 == 1, d_mid, jnp.where(rows16 == 2, d_lo, 0.0)))
        store(s * _dot_tn(split.astype(BF16), ones16) + _dot_tn(k_carry_b[rows], v_b[rows]))
    return jnp.concatenate(outs, axis=0)


def _gla_body(*refs, c_valid, n_chunks, n_seqs, has_state):
    if has_state:
        (q_ref, k_ref, v_ref, g_ref, ga_ref, wa_ref, ba_ref, gn_ref, s0_ref,
         o_ref, sout_ref, s_scr) = refs
    else:
        (q_ref, k_ref, v_ref, g_ref, ga_ref, wa_ref, ba_ref, gn_ref,
         o_ref, sout_ref, s_scr) = refs
        s0_ref = None
    blk = pl.program_id(2)
    cp = GLA_CHUNK
    total = n_seqs * n_chunks
    group = min(total, 4)
    assert total % group == 0 and (n_chunks % group == 0 or group % n_chunks == 0)

    @pl.when(blk == 0)
    def _():
        if has_state:
            s_scr[...] = s0_ref[:, 0]
        else:
            s_scr[...] = jnp.zeros(s_scr.shape, F32)

    wa_hi, wa_lo = _split_bf16(wa_ref[...])
    ga_hi, ga_lo = _split_bf16(ga_ref[...])
    pre = _dot(ga_hi, wa_hi) + _dot(ga_hi, wa_lo) + _dot(ga_lo, wa_hi) + ba_ref[...]
    la_all = _log_sigmoid(pre) * (1.0 / GLA_TAU)

    def padded(x, j0):
        if c_valid == cp:
            return x[j0 * cp:(j0 + group) * cp]
        zero = jnp.zeros((cp - c_valid, x.shape[1]), F32)
        parts = []
        for j in range(j0, j0 + group):
            parts += [x[j * c_valid:(j + 1) * c_valid], zero]
        return jnp.concatenate(parts, axis=0)

    def chunk_state(j0):
        def access(c):
            sq = (j0 + c) // n_chunks

            def store(val):
                s_scr[sq] = val
            return (lambda: s_scr[sq]), store
        return access

    q_all, k_all, v_all = (x[...].astype(F32) for x in (q_ref, k_ref, v_ref))
    outs = []
    for j0 in range(0, total, group):
        o = _gla_block(padded(q_all, j0), padded(k_all, j0), padded(v_all, j0), padded(la_all, j0),
                       chunk_state(j0), group)
        outs += [o[c * cp:c * cp + c_valid] for c in range(group)] if c_valid != cp else [o]
    o = outs[0] if len(outs) == 1 else jnp.concatenate(outs, axis=0)
    gate = g_ref[...].astype(F32)
    o_ref[...] = (_rms(o, gn_ref[...]) * (gate * (1.0 / (1.0 + jnp.exp(-gate))))).astype(o_ref.dtype)

    @pl.when(blk == pl.num_programs(2) - 1)
    def _():
        sout_ref[:, 0] = s_scr[...]


def _gla(zg, zga, w_a2p, b_a, gla_norm, state0, *, batch, seq, c_valid, n_chunks, n_seqs):
    t = zg.shape[0]
    tb = n_seqs * n_chunks * c_valid
    assert (batch * seq) == t and (n_seqs * seq) % tb == 0
    nb = (n_seqs * seq) // tb
    bg = batch // n_seqs
    has_state = state0 is not None

    def rows(col):
        return lambda b, h, j: (b * nb + j, col(h))

    in_specs = [
        pl.BlockSpec((tb, GLA_DK), rows(lambda h: h)),
        pl.BlockSpec((tb, GLA_DK), rows(lambda h: GLA_HEADS + h)),
        pl.BlockSpec((tb, GLA_DV), rows(lambda h: GLA_HEADS + h)),
        pl.BlockSpec((tb, GLA_DV), rows(lambda h: 2 * GLA_HEADS + h)),
        pl.BlockSpec((tb, LANES), rows(lambda h: 0)),
        pl.BlockSpec((LANES, GLA_DK), lambda b, h, j: (0, h)),
        pl.BlockSpec((1, GLA_DK), lambda b, h, j: (0, h)),
        pl.BlockSpec((1, GLA_DV), lambda b, h, j: (0, 0)),
    ]
    args = [zg, zg, zg, zg, zga, w_a2p, b_a.reshape(1, -1), gla_norm.reshape(1, -1)]
    state_spec = pl.BlockSpec((n_seqs, 1, GLA_DK, GLA_DV), lambda b, h, j: (b, h, 0, 0))
    if has_state:
        in_specs.append(state_spec)
        args.append(state0)
    state_bytes = n_seqs * GLA_DK * GLA_DV * 4
    vmem = (2 * tb * (2 * GLA_DK + 2 * GLA_DV + LANES) * 4 + 2 * tb * GLA_DV * 2
            + (5 if has_state else 3) * state_bytes + 16 * 1024 * 1024)
    body = functools.partial(_gla_body, c_valid=c_valid, n_chunks=n_chunks, n_seqs=n_seqs,
                             has_state=has_state)
    return pl.pallas_call(
        body,
        grid=(bg, GLA_HEADS, nb),
        in_specs=in_specs,
        out_specs=[pl.BlockSpec((tb, GLA_DV), rows(lambda h: h)), state_spec],
        out_shape=[jax.ShapeDtypeStruct((t, GLA_WIDTH), BF16),
                   jax.ShapeDtypeStruct((batch, GLA_HEADS, GLA_DK, GLA_DV), F32)],
        scratch_shapes=[pltpu.VMEM((n_seqs, GLA_DK, GLA_DV), F32)],
        compiler_params=_cparams(("parallel", "parallel", "arbitrary"), vmem),
        name="gla",
    )(*args)


def _lambda(lq1_ref, lk1_ref, lq2_ref, lk2_ref, lam_init):
    a = jnp.sum(lq1_ref[...] * lk1_ref[...], axis=-1, keepdims=True)
    b = jnp.sum(lq2_ref[...] * lk2_ref[...], axis=-1, keepdims=True)
    return jnp.exp(a) - jnp.exp(b) + lam_init


def _lane_tile(x, n):
    return x if n == 1 else jnp.concatenate([x] * n, axis=1)


SOFTMAX_ROWS = 64


def _diff_prompt_body(q_ref, k_ref, v_ref, lq1_ref, lk1_ref, lq2_ref, lk2_ref, dn_ref, o_ref,
                      s_scr, p_scr, corr_scr, m_scr, l_scr, acc_scr, *, tq, tk, lam_init):
    qi = pl.program_id(2)
    dh = DIFF_HEAD_DIM
    rs = SOFTMAX_ROWS
    q = q_ref[...].astype(F32) * (dh ** -0.5)
    q_maps = (q[:, :dh].astype(BF16), q[:, dh:].astype(BF16))
    m_scr[...] = jnp.full(m_scr.shape, -jnp.inf, F32)
    l_scr[...] = jnp.zeros(l_scr.shape, F32)
    acc_scr[...] = jnp.zeros(acc_scr.shape, F32)
    n_kv = (qi * tq + tq + tk - 1) // tk
    row = lax.broadcasted_iota(jnp.int32, (rs, tk), 0)
    kcol = lax.broadcasted_iota(jnp.int32, (rs, tk), 1)

    def step(j, masked):
        k0 = pl.multiple_of(j * tk, tk)
        kb = k_ref[pl.ds(k0, tk), :]
        v_b = v_ref[pl.ds(k0, tk), :].astype(BF16)
        for c in range(2):
            s_scr[c] = _dot_nt(q_maps[c], kb[:, c * dh:(c + 1) * dh].astype(BF16))
        for c in range(2):
            for r0 in range(0, tq, rs):
                rows = slice(r0, r0 + rs)
                s = s_scr[c, rows, :]
                if masked:
                    s = jnp.where(kcol + j * tk <= row + (qi * tq + r0), s, -jnp.inf)
                m_old = m_scr[c, rows, :]
                m_new = jnp.maximum(m_old, jnp.max(s, axis=-1, keepdims=True))
                corr = jnp.exp(m_old - m_new)
                p = jnp.exp(s - _lane_tile(m_new, tk // LANES))
                l_scr[c, rows, :] = l_scr[c, rows, :] * corr + jnp.sum(p, axis=-1, keepdims=True)
                m_scr[c, rows, :] = m_new
                corr_scr[c, rows, :] = corr
                p_scr[c, rows, :] = p.astype(BF16)
        for c in range(2):
            acc_scr[c] = acc_scr[c] * _lane_tile(corr_scr[c], DIFF_VDIM // LANES) + _dot(p_scr[c], v_b)

    def full_step(j, carry):
        step(j, False)
        return carry

    lax.fori_loop(0, n_kv - 1, full_step, 0)
    step(n_kv - 1, True)
    lam = _lambda(lq1_ref, lk1_ref, lq2_ref, lk2_ref, lam_init)
    o = acc_scr[0] / l_scr[0][:, :1] - lam * (acc_scr[1] / l_scr[1][:, :1])
    o_ref[...] = (_rms(o, dn_ref[...]) * (1.0 - lam_init)).astype(o_ref.dtype)


def _diff_prompt(zq, zk, zv, lams, diff_norm, *, batch, seq, lam_init, tq=512, tk=512):
    t = zq.shape[0]
    nq = seq // tq
    hw = 2 * DIFF_HEAD_DIM
    vec = pl.BlockSpec((1, DIFF_HEAD_DIM), lambda b, h, i: (0, 0))
    kv_spec = pl.BlockSpec((seq, hw), lambda b, h, i: (b, h))
    vmem = 2 * tq * hw * 4 + 4 * seq * hw * 4 + 2 * tq * hw * 2 + 4 * tq * LANES * 4 + 2 * tq * hw * 4 \
        + 8 * tq * tk * 4
    body = functools.partial(_diff_prompt_body, tq=tq, tk=tk, lam_init=lam_init)
    return pl.pallas_call(
        body,
        grid=(batch, DIFF_HEADS, nq),
        in_specs=[pl.BlockSpec((tq, hw), lambda b, h, i: (b * nq + i, h)), kv_spec, kv_spec,
                  vec, vec, vec, vec, pl.BlockSpec((1, DIFF_VDIM), lambda b, h, i: (0, 0))],
        out_specs=pl.BlockSpec((tq, DIFF_VDIM), lambda b, h, i: (b * nq + i, h)),
        out_shape=jax.ShapeDtypeStruct((t, DIFF_WIDTH), BF16),
        scratch_shapes=[pltpu.VMEM((2, tq, tk), F32), pltpu.VMEM((2, tq, tk), BF16),
                        pltpu.VMEM((2, tq, LANES), F32), pltpu.VMEM((2, tq, LANES), F32),
                        pltpu.VMEM((2, tq, LANES), F32), pltpu.VMEM((2, tq, DIFF_VDIM), F32)],
        compiler_params=_cparams(("parallel", "parallel", "arbitrary"), vmem),
        name="diff_attn_prompt",
    )(zq, zk, zv, *lams, diff_norm.reshape(1, -1))


def _diff_sample_body(pt_ref, q_ref, kn_ref, vn_ref, *rest, n_tok, pages_per_step, lam_init):
    kp_refs = rest[:pages_per_step]
    vp_refs = rest[pages_per_step:2 * pages_per_step]
    (lq1_ref, lk1_ref, lq2_ref, lk2_ref, dn_ref, o_ref,
     q_scr, m_scr, l_scr, acc_scr) = rest[2 * pages_per_step:]
    del pt_ref
    step = pl.program_id(1)
    dh = DIFF_HEAD_DIM
    hw = 2 * dh
    n_half = DIFF_HEADS * n_tok
    page_rows = PAGE_SIZE * DIFF_HEADS
    tok_bits = n_tok.bit_length() - 1

    def row_head(r):
        return jnp.bitwise_and(jnp.right_shift(r, tok_bits), DIFF_HEADS - 1)

    @pl.when(step == 0)
    def _():
        q = q_ref[...] * (dh ** -0.5)
        kn = kn_ref[...]
        vn = vn_ref[...]
        zpad_k = jnp.zeros((PAGE_SIZE - n_half, dh), F32)
        s_parts = []
        for c in range(2):
            qc = jnp.concatenate([q[:, h * hw + c * dh:h * hw + (c + 1) * dh] for h in range(DIFF_HEADS)],
                                 axis=0).astype(BF16)
            q_scr[c] = qc
            knc = jnp.concatenate([kn[:, h * hw + c * dh:h * hw + (c + 1) * dh] for h in range(DIFF_HEADS)]
                                  + [zpad_k], axis=0).astype(BF16)
            s_parts.append(_dot_nt(qc, knc))
        s = jnp.concatenate(s_parts, axis=0)
        r = lax.broadcasted_iota(jnp.int32, s.shape, 0)
        col = lax.broadcasted_iota(jnp.int32, s.shape, 1)
        visible = jnp.logical_and(jnp.right_shift(col, tok_bits) == row_head(r),
                                  jnp.bitwise_and(col, n_tok - 1) <= jnp.bitwise_and(r, n_tok - 1))
        s = jnp.where(visible, s, -jnp.inf)
        m = jnp.max(s, axis=-1, keepdims=True)
        p = jnp.exp(s - m)
        m_scr[...] = jnp.broadcast_to(m, m_scr.shape)
        l_scr[...] = jnp.broadcast_to(jnp.sum(p, axis=-1, keepdims=True), l_scr.shape)
        vn_rows = jnp.concatenate([vn[:, h * hw:(h + 1) * hw] for h in range(DIFF_HEADS)]
                                  + [jnp.zeros((PAGE_SIZE - n_half, hw), F32)], axis=0).astype(BF16)
        acc_scr[...] = _dot(p.astype(BF16), vn_rows)

    q0 = q_scr[0]
    q1 = q_scr[1]

    def page_scores(kp_ref):
        k0 = kp_ref[0, pl.ds(0, page_rows, stride=2), :].astype(BF16)
        k1 = kp_ref[0, pl.ds(1, page_rows, stride=2), :].astype(BF16)
        return jnp.concatenate([_dot_nt(q0, k0), _dot_nt(q1, k1)], axis=0)

    r = lax.broadcasted_iota(jnp.int32, (2 * n_half, page_rows), 0)
    col = lax.broadcasted_iota(jnp.int32, (2 * n_half, page_rows), 1)
    own_head = jnp.bitwise_and(col, DIFF_HEADS - 1) == row_head(r)
    m, l, acc = m_scr[...], l_scr[...], acc_scr[...]
    s_next = page_scores(kp_refs[0])
    for i in range(pages_per_step):
        s = jnp.where(own_head, s_next, -jnp.inf)
        if i + 1 < pages_per_step:
            s_next = page_scores(kp_refs[i + 1])
        m_new = jnp.maximum(m, jnp.max(s, axis=-1, keepdims=True))
        corr = jnp.exp(m - m_new)
        p = jnp.exp(s - _lane_tile(m_new, page_rows // LANES))
        l = l * corr + jnp.sum(p, axis=-1, keepdims=True)
        acc = acc * _lane_tile(corr, DIFF_VDIM // LANES) + _dot(p.astype(BF16), vp_refs[i][0].astype(BF16))
        m = m_new
    m_scr[...] = m
    l_scr[...] = l
    acc_scr[...] = acc

    @pl.when(step == pl.num_programs(1) - 1)
    def _():
        lam = _lambda(lq1_ref, lk1_ref, lq2_ref, lk2_ref, lam_init)
        o_all = acc_scr[...] / l_scr[...][:, :1]
        dn = dn_ref[...]
        outs = []
        for h in range(DIFF_HEADS):
            o = o_all[h * n_tok:(h + 1) * n_tok] - lam * o_all[n_half + h * n_tok:n_half + (h + 1) * n_tok]
            outs.append(_rms(o, dn) * (1.0 - lam_init))
        o_ref[...] = jnp.concatenate(outs, axis=1).astype(o_ref.dtype)


def _diff_sample(zq, zk, zv, cache_k, cache_v, page_table, lams, diff_norm, *, n_tok, lam_init,
                 pages_per_step=8):
    t = zq.shape[0]
    batch, n_pages = page_table.shape
    assert n_pages % pages_per_step == 0 and t == batch * n_tok
    assert n_tok & (n_tok - 1) == 0 and n_tok % 8 == 0 and DIFF_HEADS * n_tok <= PAGE_SIZE
    n_steps = n_pages // pages_per_step
    dh = DIFF_HEAD_DIM
    hw = 2 * dh
    n_half = DIFF_HEADS * n_tok
    n_rows = 2 * n_half
    page_rows = PAGE_SIZE * DIFF_HEADS
    tok_spec = pl.BlockSpec((n_tok, DIFF_WIDTH), lambda b, s, pt: (b, 0))

    def page_spec(shape, i):
        return pl.BlockSpec((1,) + shape,
                            lambda b, s, pt: (pt[b * n_pages + s * pages_per_step + i], 0, 0))

    vec = pl.BlockSpec((1, dh), lambda b, s, pt: (0, 0))
    in_specs = ([tok_spec, tok_spec, tok_spec]
                + [page_spec((2 * page_rows, dh), i) for i in range(pages_per_step)]
                + [page_spec((page_rows, hw), i) for i in range(pages_per_step)]
                + [vec, vec, vec, vec, pl.BlockSpec((1, DIFF_VDIM), lambda b, s, pt: (0, 0))])
    page_bytes = page_rows * hw * 4
    vmem = 4 * pages_per_step * page_bytes + 2 * pages_per_step * page_bytes \
        + 6 * n_rows * pages_per_step * page_rows * 4 + 4 * 1024 * 1024
    body = functools.partial(_diff_sample_body, n_tok=n_tok, pages_per_step=pages_per_step,
                             lam_init=lam_init)
    grid_spec = pltpu.PrefetchScalarGridSpec(
        num_scalar_prefetch=1,
        grid=(batch, n_steps),
        in_specs=in_specs,
        out_specs=pl.BlockSpec((n_tok, DIFF_WIDTH), lambda b, s, pt: (b, 0)),
        scratch_shapes=[pltpu.VMEM((2, n_half, dh), BF16), pltpu.VMEM((n_rows, LANES), F32),
                        pltpu.VMEM((n_rows, LANES), F32), pltpu.VMEM((n_rows, DIFF_VDIM), F32)],
    )
    return pl.pallas_call(
        body,
        grid_spec=grid_spec,
        out_shape=jax.ShapeDtypeStruct((t, DIFF_WIDTH), F32),
        compiler_params=_cparams(("parallel", "arbitrary"), vmem),
        name="diff_attn_sample",
    )(page_table.reshape(-1), zq, zk, zv, *([cache_k] * pages_per_step), *([cache_v] * pages_per_step),
      *lams, diff_norm.reshape(1, -1))


def _layer(x, p, lam_init, *, batch, seq, gla_state, attend):
    h = _rmsnorm(x, p['pre_mix_norm'], BF16)
    wt = p['w_in_t']
    zg = _matmul_nt(h, wt, row0=0, n_out=GLA_COLS, tn=512, out_dtype=BF16)
    zga = _matmul_nt(h, wt, row0=GATE_COL0, n_out=LANES, tn=LANES, out_dtype=F32)
    zq = _matmul_nt(h, wt, row0=DIFF_COL0, n_out=DIFF_WIDTH, tn=512,
                    out_dtype=BF16 if seq % 16 == 0 else F32)
    zk = _matmul_nt(h, wt, row0=DIFF_COL0 + DIFF_WIDTH, n_out=DIFF_WIDTH, tn=512, out_dtype=F32)
    zv = _matmul_nt(h, wt, row0=DIFF_COL0 + 2 * DIFF_WIDTH, n_out=DIFF_WIDTH, tn=512, out_dtype=F32)
    if gla_state is None:
        o_gla, s_out = _gla(zg, zga, p['w_a2p'], p['b_a'], p['gla_norm'], None, batch=batch, seq=seq,
                            c_valid=GLA_CHUNK, n_chunks=8, n_seqs=1)
    else:
        o_gla, s_out = _gla(zg, zga, p['w_a2p'], p['b_a'], p['gla_norm'], gla_state, batch=batch,
                            seq=seq, c_valid=seq, n_chunks=1, n_seqs=4)
    o_diff = attend(zq, zk, zv)
    mixed = _matmul2(o_gla, o_diff, p['w_o'], tn=512)
    x1, h2 = _residual_norm2(x, mixed, p['post_mix_norm'], p['pre_ffn_norm'])
    act = _gate_up(h2, p['w_gate'], p['w_up'], tn=256)
    f = _matmul(act, p['w_down'], col0=0, n_out=D_MODEL, tn=256, out_dtype=F32)
    y = _residual_norm(x1, f, p['post_ffn_norm'])
    return y, zk, zv, s_out


def kernel(x_prompt, x_sample, cache_k, cache_v, state_gla, page_table, pre_mix_norm, w_in, w_a2, b_a, gla_norm, lambda_q1, lambda_k1, lambda_q2, lambda_k2, diff_norm, w_o, post_mix_norm, pre_ffn_norm, w_gate, w_up, w_down, post_ffn_norm):
    depth = w_in.shape[0]
    bp, sp, d = x_prompt.shape
    bs, ss, _ = x_sample.shape
    n_pool = cache_k.shape[1]
    xp = x_prompt.reshape(bp * sp, d)
    xs = x_sample.reshape(bs * ss, d)
    outs = [[] for _ in range(6)]
    for l in range(depth):
        lam_init = 0.8 - 0.6 * math.exp(-0.3 * l)
        p = {
            'pre_mix_norm': pre_mix_norm[l], 'w_in_t': jnp.swapaxes(w_in[l], 0, 1),
            'w_a2p': jnp.pad(w_a2[l], ((0, LANES - GLA_GATE_RANK), (0, 0))),
            'b_a': b_a[l], 'gla_norm': gla_norm[l], 'w_o': w_o[l],
            'post_mix_norm': post_mix_norm[l], 'pre_ffn_norm': pre_ffn_norm[l],
            'w_gate': w_gate[l], 'w_up': w_up[l], 'w_down': w_down[l],
            'post_ffn_norm': post_ffn_norm[l],
        }
        lams = [v[l].reshape(1, -1) for v in (lambda_q1, lambda_k1, lambda_q2, lambda_k2)]
        ck = cache_k[l].reshape(n_pool, 2 * PAGE_SIZE * DIFF_HEADS, DIFF_HEAD_DIM)
        cv = cache_v[l].reshape(n_pool, PAGE_SIZE * DIFF_HEADS, DIFF_VDIM)
        attend_p = functools.partial(_diff_prompt, lams=lams, diff_norm=diff_norm[l], batch=bp, seq=sp,
                                     lam_init=lam_init)
        attend_s = functools.partial(_diff_sample, cache_k=ck, cache_v=cv, page_table=page_table,
                                     lams=lams, diff_norm=diff_norm[l], n_tok=ss, lam_init=lam_init)
        xp, kp, vp, s_p = _layer(xp, p, lam_init, batch=bp, seq=sp, gla_state=None, attend=attend_p)
        xs, ks, vs, s_s = _layer(xs, p, lam_init, batch=bs, seq=ss, gla_state=state_gla[l],
                                 attend=attend_s)
        outs[0].append(kp.reshape(bp, sp, DIFF_HEADS, 2, DIFF_HEAD_DIM))
        outs[1].append(vp.reshape(bp, sp, DIFF_HEADS, DIFF_VDIM))
        outs[2].append(s_p)
        outs[3].append(ks.reshape(bs, ss, DIFF_HEADS, 2, DIFF_HEAD_DIM))
        outs[4].append(vs.reshape(bs, ss, DIFF_HEADS, DIFF_VDIM))
        outs[5].append(s_s)
    k_p, v_p, st_p, k_s, v_s, st_s = [o[0][None] if depth == 1 else jnp.stack(o) for o in outs]
    return (xp.reshape(bp, sp, d), xs.reshape(bs, ss, d), k_p, v_p, st_p, k_s, v_s, st_s)
```

```python
import functools
import math

import jax
import jax.numpy as jnp
from jax import lax
from jax.experimental import pallas as pl
from jax.experimental.pallas import tpu as pltpu

F32 = jnp.float32
BF16 = jnp.bfloat16

D_MODEL = 4096
GLA_HEADS = 4
GLA_DK = 256
GLA_DV = 512
GLA_KEY_WIDTH = GLA_HEADS * GLA_DK
GLA_WIDTH = GLA_HEADS * GLA_DV
GLA_GATE_RANK = 16
GLA_TAU = 16.0
GLA_CHUNK = 64
DIFF_HEADS = 8
DIFF_HEAD_DIM = 128
DIFF_VDIM = 256
DIFF_WIDTH = DIFF_HEADS * DIFF_VDIM
PAGE_SIZE = 128
EPS = 1e-6

GLA_COLS = 2 * GLA_KEY_WIDTH + 2 * GLA_WIDTH
GATE_COL0 = GLA_COLS
DIFF_COL0 = GLA_COLS + GLA_GATE_RANK

LANES = 128
V7X_VMEM_BYTES = 64 * 1024 * 1024
VMEM_HEADROOM_BYTES = 6 * 1024 * 1024

ROW_TILE = 1024
GATE_UP_ROW_TILE = 2048
NORM_ROWS = 256


def _cparams(semantics, vmem_bytes):
    limit = min(int(vmem_bytes) + VMEM_HEADROOM_BYTES, V7X_VMEM_BYTES - 2 * 1024 * 1024)
    return pltpu.CompilerParams(dimension_semantics=semantics, vmem_limit_bytes=limit)


def _rms(x, gain):
    ms = jnp.mean(x * x, axis=-1, keepdims=True)
    return x * lax.rsqrt(ms + EPS) * gain


def _dot(a, b):
    return jnp.dot(a, b, preferred_element_type=F32)


def _dot_nt(a, b):
    return lax.dot_general(a, b, (((1,), (1,)), ((), ())), preferred_element_type=F32)


def _dot_tn(a, b):
    return lax.dot_general(a, b, (((0,), (0,)), ((), ())), preferred_element_type=F32)


def _split_bf16(x):
    hi = x.astype(BF16)
    lo = (x - hi.astype(F32)).astype(BF16)
    return hi, lo


def _rmsnorm_body(x_ref, g_ref, o_ref):
    o_ref[...] = _rms(x_ref[...], g_ref[...]).astype(o_ref.dtype)


def _rmsnorm(x, gain, out_dtype):
    t, d = x.shape
    tr = NORM_ROWS
    vmem = 2 * tr * d * (4 + jnp.dtype(out_dtype).itemsize) + 4 * tr * d
    return pl.pallas_call(
        _rmsnorm_body,
        grid=(t // tr,),
        in_specs=[pl.BlockSpec((tr, d), lambda i: (i, 0)),
                  pl.BlockSpec((1, d), lambda i: (0, 0))],
        out_specs=pl.BlockSpec((tr, d), lambda i: (i, 0)),
        out_shape=jax.ShapeDtypeStruct((t, d), out_dtype),
        compiler_params=_cparams(("parallel",), vmem),
        name="rmsnorm",
    )(x, gain.reshape(1, d))


def _residual_norm2_body(x_ref, y_ref, g1_ref, g2_ref, x1_ref, h_ref):
    x1 = x_ref[...] + _rms(y_ref[...], g1_ref[...])
    x1_ref[...] = x1
    h_ref[...] = _rms(x1, g2_ref[...]).astype(h_ref.dtype)


def _residual_norm2(x, y, g1, g2):
    t, d = x.shape
    tr = NORM_ROWS
    vmem = 2 * tr * d * (4 + 4 + 4 + 2) + 8 * tr * d
    row = pl.BlockSpec((tr, d), lambda i: (i, 0))
    vec = pl.BlockSpec((1, d), lambda i: (0, 0))
    return pl.pallas_call(
        _residual_norm2_body,
        grid=(t // tr,),
        in_specs=[row, row, vec, vec],
        out_specs=[row, row],
        out_shape=[jax.ShapeDtypeStruct((t, d), F32), jax.ShapeDtypeStruct((t, d), BF16)],
        compiler_params=_cparams(("parallel",), vmem),
        name="residual_norm2",
    )(x, y, g1.reshape(1, d), g2.reshape(1, d))


def _residual_norm_body(x_ref, y_ref, g_ref, o_ref):
    o_ref[...] = x_ref[...] + _rms(y_ref[...], g_ref[...])


def _residual_norm(x, y, gain):
    t, d = x.shape
    tr = NORM_ROWS
    vmem = 2 * tr * d * 12 + 8 * tr * d
    row = pl.BlockSpec((tr, d), lambda i: (i, 0))
    vec = pl.BlockSpec((1, d), lambda i: (0, 0))
    return pl.pallas_call(
        _residual_norm_body,
        grid=(t // tr,),
        in_specs=[row, row, vec],
        out_specs=row,
        out_shape=jax.ShapeDtypeStruct((t, d), F32),
        compiler_params=_cparams(("parallel",), vmem),
        name="residual_norm",
    )(x, y, gain.reshape(1, d))


def _mm_body(a_ref, w_ref, o_ref):
    o_ref[...] = _dot(a_ref[...].astype(BF16), w_ref[...].astype(BF16)).astype(o_ref.dtype)


def _matmul(a, w, *, col0, n_out, tn, out_dtype):
    m, k = a.shape
    tm = min(ROW_TILE, m)
    assert m % tm == 0 and n_out % tn == 0 and col0 % tn == 0 and tn % LANES == 0
    cb0 = col0 // tn
    a_bytes = jnp.dtype(a.dtype).itemsize
    o_bytes = jnp.dtype(out_dtype).itemsize
    vmem = tm * k * a_bytes + 2 * k * tn * 4 + k * tn * 2 + 2 * tm * tn * o_bytes + tm * tn * 4
    return pl.pallas_call(
        _mm_body,
        grid=(m // tm, n_out // tn),
        in_specs=[pl.BlockSpec((tm, k), lambda i, j: (i, 0), pipeline_mode=pl.Buffered(1)),
                  pl.BlockSpec((k, tn), lambda i, j: (0, j + cb0))],
        out_specs=pl.BlockSpec((tm, tn), lambda i, j: (i, j)),
        out_shape=jax.ShapeDtypeStruct((m, n_out), out_dtype),
        compiler_params=_cparams(("parallel", "arbitrary"), vmem),
        name="proj",
    )(a, w)


def _mm_nt_body(a_ref, wt_ref, o_ref):
    o_ref[...] = _dot_nt(a_ref[...].astype(BF16), wt_ref[...].astype(BF16)).astype(o_ref.dtype)


def _matmul_nt(a, wt, *, row0, n_out, tn, out_dtype):
    m, k = a.shape
    tm = min(ROW_TILE, m)
    assert m % tm == 0 and n_out % tn == 0 and tn % LANES == 0 and row0 % 8 == 0
    assert wt.shape[1] == k and row0 + n_out <= wt.shape[0]
    a_bytes = jnp.dtype(a.dtype).itemsize
    o_bytes = jnp.dtype(out_dtype).itemsize
    vmem = 2 * tm * k * a_bytes + 2 * k * tn * 4 + k * tn * 2 + 2 * tm * tn * o_bytes + tm * tn * 4
    return pl.pallas_call(
        _mm_nt_body,
        grid=(m // tm, n_out // tn),
        in_specs=[pl.BlockSpec((tm, k), lambda i, j: (i, 0)),
                  pl.BlockSpec((pl.Element(tn), pl.Element(k)),
                               lambda i, j: (pl.multiple_of(row0 + j * tn, 8), 0))],
        out_specs=pl.BlockSpec((tm, tn), lambda i, j: (i, j)),
        out_shape=jax.ShapeDtypeStruct((m, n_out), out_dtype),
        compiler_params=_cparams(("parallel", "arbitrary"), vmem),
        name="proj_in",
    )(a, wt)


def _mm_nt_keys_body(a_ref, wt_ref, o_ref, rows_ref):
    res = _dot_nt(a_ref[...].astype(BF16), wt_ref[...].astype(BF16))
    o_ref[...] = res
    tm, tn = res.shape
    per_tile = tn // LANES
    groups = rows_ref.shape[0] // tm
    j = pl.program_id(1)
    for g in range(per_tile):
        rows_ref[pl.ds(j * per_tile + g, tm, stride=groups), :] = res[:, g * LANES:(g + 1) * LANES]


def _matmul_nt_keys(a, wt, *, row0, n_out, tn):
    m, k = a.shape
    tm = min(ROW_TILE, m)
    groups = n_out // LANES
    assert m % tm == 0 and n_out % tn == 0 and tn % LANES == 0 and row0 % 8 == 0
    a_bytes = jnp.dtype(a.dtype).itemsize
    vmem = tm * k * a_bytes + 2 * k * tn * 4 + k * tn * 2 + 3 * tm * tn * 4 + 2 * tm * n_out * 4
    return pl.pallas_call(
        _mm_nt_keys_body,
        grid=(m // tm, n_out // tn),
        in_specs=[pl.BlockSpec((tm, k), lambda i, j: (i, 0), pipeline_mode=pl.Buffered(1)),
                  pl.BlockSpec((pl.Element(tn), pl.Element(k)),
                               lambda i, j: (pl.multiple_of(row0 + j * tn, 8), 0))],
        out_specs=[pl.BlockSpec((tm, tn), lambda i, j: (i, j)),
                   pl.BlockSpec((tm * groups, LANES), lambda i, j: (i, 0))],
        out_shape=[jax.ShapeDtypeStruct((m, n_out), F32),
                   jax.ShapeDtypeStruct((m * groups, LANES), F32)],
        compiler_params=_cparams(("parallel", "arbitrary"), vmem),
        name="proj_keys",
    )(a, wt)


def _mm2_body(a1_ref, a2_ref, w_ref, o_ref):
    k1 = a1_ref.shape[1]
    w = w_ref[...].astype(BF16)
    o_ref[...] = (_dot(a1_ref[...].astype(BF16), w[:k1]) + _dot(a2_ref[...].astype(BF16), w[k1:]))


def _matmul2(a1, a2, w, *, tn):
    m, k1 = a1.shape
    k2 = a2.shape[1]
    k, n = w.shape
    assert k == k1 + k2 and n % tn == 0
    tm = min(ROW_TILE, m)
    vmem = (2 * tm * k1 * jnp.dtype(a1.dtype).itemsize + 2 * tm * k2 * jnp.dtype(a2.dtype).itemsize
            + 2 * k * tn * 4 + k * tn * 2 + 3 * tm * tn * 4)
    return pl.pallas_call(
        _mm2_body,
        grid=(m // tm, n // tn),
        in_specs=[pl.BlockSpec((tm, k1), lambda i, j: (i, 0)),
                  pl.BlockSpec((tm, k2), lambda i, j: (i, 0)),
                  pl.BlockSpec((k, tn), lambda i, j: (0, j))],
        out_specs=pl.BlockSpec((tm, tn), lambda i, j: (i, j)),
        out_shape=jax.ShapeDtypeStruct((m, n), F32),
        compiler_params=_cparams(("parallel", "arbitrary"), vmem),
        name="out_proj",
    )(a1, a2, w)


def _gate_up_body(a_ref, wg_ref, wu_ref, o_ref):
    a = a_ref[...]
    g = _dot(a, wg_ref[...].astype(BF16))
    u = _dot(a, wu_ref[...].astype(BF16))
    o_ref[...] = (g * (1.0 / (1.0 + jnp.exp(-g))) * u).astype(o_ref.dtype)


def _gate_up(a, w_gate, w_up, *, tn):
    m, k = a.shape
    n = w_gate.shape[1]
    assert n % tn == 0
    tm = min(GATE_UP_ROW_TILE, m)
    vmem = tm * k * 2 + 4 * k * tn * 4 + 2 * k * tn * 2 + 2 * tm * tn * 2 + 3 * tm * tn * 4
    wspec = pl.BlockSpec((k, tn), lambda i, j: (0, j))
    return pl.pallas_call(
        _gate_up_body,
        grid=(m // tm, n // tn),
        in_specs=[pl.BlockSpec((tm, k), lambda i, j: (i, 0), pipeline_mode=pl.Buffered(1)), wspec, wspec],
        out_specs=pl.BlockSpec((tm, tn), lambda i, j: (i, j)),
        out_shape=jax.ShapeDtypeStruct((m, n), BF16),
        compiler_params=_cparams(("parallel", "arbitrary"), vmem),
        name="ffn_gate_up",
    )(a, w_gate, w_up)


def _log_sigmoid(x):
    return -(jnp.maximum(-x, 0.0) + jnp.log1p(jnp.exp(-jnp.abs(x))))


def _gla_block(q, k, v, la, chunk_state, n_chunks):
    cp = GLA_CHUNK
    r = n_chunks * cp
    shift = cp.bit_length() - 1
    ri = lax.broadcasted_iota(jnp.int32, (r, r), 0)
    ci = lax.broadcasted_iota(jnp.int32, (r, r), 1)
    tril_mask = jnp.logical_and(jnp.right_shift(ri, shift) == jnp.right_shift(ci, shift), ri >= ci)
    tril_b = jnp.where(tril_mask, 1.0, 0.0).astype(BF16)
    la_hi, la_lo = _split_bf16(la)
    b = _dot(tril_b, la_hi) + _dot(tril_b, la_lo)
    b_last = [b[(c + 1) * cp - 1:(c + 1) * cp, :] for c in range(n_chunks)]
    b_end = jnp.concatenate([jnp.broadcast_to(bl, (cp, GLA_DK)) for bl in b_last], axis=0)
    q_dec = (q * (GLA_DK ** -0.5) * jnp.exp(b)).astype(BF16)
    k_inv = (k * jnp.exp(-b)).astype(BF16)
    k_carry = k * jnp.exp(b_end - b)
    scores = jnp.where(tril_mask, _dot_nt(q_dec, k_inv), 0.0).astype(BF16)
    v_b = v.astype(BF16)
    k_carry_b = k_carry.astype(BF16)
    o_intra = _dot(scores, v_b)
    rows16 = lax.broadcasted_iota(jnp.int32, (16, GLA_DK), 0)
    ones16 = jnp.ones((16, GLA_DV), BF16)
    outs = []
    for c in range(n_chunks):
        rows = slice(c * cp, (c + 1) * cp)
        load, store = chunk_state(c)
        s = load()
        outs.append(o_intra[rows] + _dot(q_dec[rows], s.astype(BF16)))
        decay = jnp.exp(b_last[c])
        d_hi = decay.astype(BF16).astype(F32)
        d_mid = (decay - d_hi).astype(BF16).astype(F32)
        d_lo = (decay - d_hi - d_mid).astype(BF16).astype(F32)
        split = jnp.where(rows16 == 0, d_hi, jnp.where(rows16 == 1, d_mid, jnp.where(rows16 == 2, d_lo, 0.0)))
        store(s * _dot_tn(split.astype(BF16), ones16) + _dot_tn(k_carry_b[rows], v_b[rows]))
    return jnp.concatenate(outs, axis=0)


def _gla_body(*refs, c_valid, n_chunks, n_seqs, has_state):
    if has_state:
        (q_ref, k_ref, v_ref, g_ref, ga_ref, wa_ref, ba_ref, gn_ref, s0_ref,
         o_ref, sout_ref, s_scr) = refs
    else:
        (q_ref, k_ref, v_ref, g_ref, ga_ref, wa_ref, ba_ref, gn_ref,
         o_ref, sout_ref, s_scr) = refs
        s0_ref = None
    blk = pl.program_id(2)
    cp = GLA_CHUNK
    total = n_seqs * n_chunks
    group = min(total, 4)
    assert total % group == 0 and (n_chunks % group == 0 or group % n_chunks == 0)

    @pl.when(blk == 0)
    def _():
        if has_state:
            s_scr[...] = s0_ref[:, 0]
        else:
            s_scr[...] = jnp.zeros(s_scr.shape, F32)

    wa_hi, wa_lo = _split_bf16(wa_ref[...])
    ga_hi, ga_lo = _split_bf16(ga_ref[...])
    pre = _dot(ga_hi, wa_hi) + _dot(ga_hi, wa_lo) + _dot(ga_lo, wa_hi) + ba_ref[...]
    la_all = _log_sigmoid(pre) * (1.0 / GLA_TAU)

    def padded(x, j0):
        if c_valid == cp:
            return x[j0 * cp:(j0 + group) * cp]
        zero = jnp.zeros((cp - c_valid, x.shape[1]), F32)
        parts = []
        for j in range(j0, j0 + group):
            parts += [x[j * c_valid:(j + 1) * c_valid], zero]
        return jnp.concatenate(parts, axis=0)

    def chunk_state(j0):
        def access(c):
            sq = (j0 + c) // n_chunks

            def store(val):
                s_scr[sq] = val
            return (lambda: s_scr[sq]), store
        return access

    q_all, k_all, v_all = (x[...].astype(F32) for x in (q_ref, k_ref, v_ref))
    outs = []
    for j0 in range(0, total, group):
        o = _gla_block(padded(q_all, j0), padded(k_all, j0), padded(v_all, j0), padded(la_all, j0),
                       chunk_state(j0), group)
        outs += [o[c * cp:c * cp + c_valid] for c in range(group)] if c_valid != cp else [o]
    o = outs[0] if len(outs) == 1 else jnp.concatenate(outs, axis=0)
    gate = g_ref[...].astype(F32)
    o_ref[...] = (_rms(o, gn_ref[...]) * (gate * (1.0 / (1.0 + jnp.exp(-gate))))).astype(o_ref.dtype)

    @pl.when(blk == pl.num_programs(2) - 1)
    def _():
        sout_ref[:, 0] = s_scr[...]


def _gla(zg, zga, w_a2p, b_a, gla_norm, state0, *, batch, seq, c_valid, n_chunks, n_seqs):
    t = zg.shape[0]
    tb = n_seqs * n_chunks * c_valid
    assert (batch * seq) == t and (n_seqs * seq) % tb == 0
    nb = (n_seqs * seq) // tb
    bg = batch // n_seqs
    has_state = state0 is not None

    def rows(col):
        return lambda b, h, j: (b * nb + j, col(h))

    in_specs = [
        pl.BlockSpec((tb, GLA_DK), rows(lambda h: h)),
        pl.BlockSpec((tb, GLA_DK), rows(lambda h: GLA_HEADS + h)),
        pl.BlockSpec((tb, GLA_DV), rows(lambda h: GLA_HEADS + h)),
        pl.BlockSpec((tb, GLA_DV), rows(lambda h: 2 * GLA_HEADS + h)),
        pl.BlockSpec((tb, LANES), rows(lambda h: 0)),
        pl.BlockSpec((LANES, GLA_DK), lambda b, h, j: (0, h)),
        pl.BlockSpec((1, GLA_DK), lambda b, h, j: (0, h)),
        pl.BlockSpec((1, GLA_DV), lambda b, h, j: (0, 0)),
    ]
    args = [zg, zg, zg, zg, zga, w_a2p, b_a.reshape(1, -1), gla_norm.reshape(1, -1)]
    state_spec = pl.BlockSpec((n_seqs, 1, GLA_DK, GLA_DV), lambda b, h, j: (b, h, 0, 0))
    if has_state:
        in_specs.append(state_spec)
        args.append(state0)
    state_bytes = n_seqs * GLA_DK * GLA_DV * 4
    vmem = (2 * tb * (2 * GLA_DK + 2 * GLA_DV + LANES) * 4 + 2 * tb * GLA_DV * 2
            + (5 if has_state else 3) * state_bytes + 16 * 1024 * 1024)
    body = functools.partial(_gla_body, c_valid=c_valid, n_chunks=n_chunks, n_seqs=n_seqs,
                             has_state=has_state)
    return pl.pallas_call(
        body,
        grid=(bg, GLA_HEADS, nb),
        in_specs=in_specs,
        out_specs=[pl.BlockSpec((tb, GLA_DV), rows(lambda h: h)), state_spec],
        out_shape=[jax.ShapeDtypeStruct((t, GLA_WIDTH), BF16),
                   jax.ShapeDtypeStruct((batch, GLA_HEADS, GLA_DK, GLA_DV), F32)],
        scratch_shapes=[pltpu.VMEM((n_seqs, GLA_DK, GLA_DV), F32)],
        compiler_params=_cparams(("parallel", "parallel", "arbitrary"), vmem),
        name="gla",
    )(*args)


def _lambda(lq1_ref, lk1_ref, lq2_ref, lk2_ref, lam_init):
    a = jnp.sum(lq1_ref[...] * lk1_ref[...], axis=-1, keepdims=True)
    b = jnp.sum(lq2_ref[...] * lk2_ref[...], axis=-1, keepdims=True)
    return jnp.exp(a) - jnp.exp(b) + lam_init


def _lane_tile(x, n):
    return x if n == 1 else jnp.concatenate([x] * n, axis=1)


SOFTMAX_ROWS = 64


def _diff_prompt_body(q_ref, k_ref, v_ref, lq1_ref, lk1_ref, lq2_ref, lk2_ref, dn_ref, o_ref,
                      s_scr, p_scr, corr_scr, m_scr, l_scr, acc_scr, *, tq, tk, lam_init):
    qi = pl.program_id(2)
    dh = DIFF_HEAD_DIM
    rs = SOFTMAX_ROWS
    q = q_ref[...].astype(F32) * (dh ** -0.5)
    q_maps = (q[:, :dh].astype(BF16), q[:, dh:].astype(BF16))
    m_scr[...] = jnp.full(m_scr.shape, -jnp.inf, F32)
    l_scr[...] = jnp.zeros(l_scr.shape, F32)
    acc_scr[...] = jnp.zeros(acc_scr.shape, F32)
    n_kv = (qi * tq + tq + tk - 1) // tk
    row = lax.broadcasted_iota(jnp.int32, (rs, tk), 0)
    kcol = lax.broadcasted_iota(jnp.int32, (rs, tk), 1)

    def step(j, masked):
        k0 = pl.multiple_of(j * tk, tk)
        kb = k_ref[pl.ds(k0, tk), :]
        v_b = v_ref[pl.ds(k0, tk), :].astype(BF16)
        for c in range(2):
            s_scr[c] = _dot_nt(q_maps[c], kb[:, c * dh:(c + 1) * dh].astype(BF16))
        for c in range(2):
            for r0 in range(0, tq, rs):
                rows = slice(r0, r0 + rs)
                s = s_scr[c, rows, :]
                if masked:
                    s = jnp.where(kcol + j * tk <= row + (qi * tq + r0), s, -jnp.inf)
                m_old = m_scr[c, rows, :]
                m_new = jnp.maximum(m_old, jnp.max(s, axis=-1, keepdims=True))
                corr = jnp.exp(m_old - m_new)
                p = jnp.exp(s - _lane_tile(m_new, tk // LANES))
                l_scr[c, rows, :] = l_scr[c, rows, :] * corr + jnp.sum(p, axis=-1, keepdims=True)
                m_scr[c, rows, :] = m_new
                corr_scr[c, rows, :] = corr
                p_scr[c, rows, :] = p.astype(BF16)
        for c in range(2):
            acc_scr[c] = acc_scr[c] * _lane_tile(corr_scr[c], DIFF_VDIM // LANES) + _dot(p_scr[c], v_b)

    def full_step(j, carry):
        step(j, False)
        return carry

    lax.fori_loop(0, n_kv - 1, full_step, 0)
    step(n_kv - 1, True)
    lam = _lambda(lq1_ref, lk1_ref, lq2_ref, lk2_ref, lam_init)
    o = acc_scr[0] / l_scr[0][:, :1] - lam * (acc_scr[1] / l_scr[1][:, :1])
    o_ref[...] = (_rms(o, dn_ref[...]) * (1.0 - lam_init)).astype(o_ref.dtype)


def _diff_prompt(zq, zk, zv, lams, diff_norm, *, batch, seq, lam_init, tq=512, tk=512):
    t = zq.shape[0]
    nq = seq // tq
    hw = 2 * DIFF_HEAD_DIM
    vec = pl.BlockSpec((1, DIFF_HEAD_DIM), lambda b, h, i: (0, 0))
    kv_spec = pl.BlockSpec((seq, hw), lambda b, h, i: (b, h))
    vmem = 2 * tq * hw * 4 + 4 * seq * hw * 4 + 2 * tq * hw * 2 + 4 * tq * LANES * 4 + 2 * tq * hw * 4 \
        + 8 * tq * tk * 4
    body = functools.partial(_diff_prompt_body, tq=tq, tk=tk, lam_init=lam_init)
    return pl.pallas_call(
        body,
        grid=(batch, DIFF_HEADS, nq),
        in_specs=[pl.BlockSpec((tq, hw), lambda b, h, i: (b * nq + i, h)), kv_spec, kv_spec,
                  vec, vec, vec, vec, pl.BlockSpec((1, DIFF_VDIM), lambda b, h, i: (0, 0))],
        out_specs=pl.BlockSpec((tq, DIFF_VDIM), lambda b, h, i: (b * nq + i, h)),
        out_shape=jax.ShapeDtypeStruct((t, DIFF_WIDTH), BF16),
        scratch_shapes=[pltpu.VMEM((2, tq, tk), F32), pltpu.VMEM((2, tq, tk), BF16),
                        pltpu.VMEM((2, tq, LANES), F32), pltpu.VMEM((2, tq, LANES), F32),
                        pltpu.VMEM((2, tq, LANES), F32), pltpu.VMEM((2, tq, DIFF_VDIM), F32)],
        compiler_params=_cparams(("parallel", "parallel", "arbitrary"), vmem),
        name="diff_attn_prompt",
    )(zq, zk, zv, *lams, diff_norm.reshape(1, -1))


def _diff_sample_body(pt_ref, q_ref, kn_ref, vn_ref, *rest, n_tok, pages_per_step, lam_init):
    kp_refs = rest[:pages_per_step]
    vp_refs = rest[pages_per_step:2 * pages_per_step]
    (lq1_ref, lk1_ref, lq2_ref, lk2_ref, dn_ref, o_ref,
     q_scr, m_scr, l_scr, acc_scr) = rest[2 * pages_per_step:]
    del pt_ref
    step = pl.program_id(1)
    dh = DIFF_HEAD_DIM
    hw = 2 * dh
    n_half = DIFF_HEADS * n_tok
    page_rows = PAGE_SIZE * DIFF_HEADS
    tok_bits = n_tok.bit_length() - 1

    def row_head(r):
        return jnp.bitwise_and(jnp.right_shift(r, tok_bits), DIFF_HEADS - 1)

    @pl.when(step == 0)
    def _():
        q = q_ref[...] * (dh ** -0.5)
        kn = kn_ref[...]
        vn = vn_ref[...]
        zpad_k = jnp.zeros((PAGE_SIZE - n_half, dh), F32)
        s_parts = []
        for c in range(2):
            qc = jnp.concatenate([q[:, h * hw + c * dh:h * hw + (c + 1) * dh] for h in range(DIFF_HEADS)],
                                 axis=0).astype(BF16)
            q_scr[c] = qc
            knc = jnp.concatenate([kn[:, h * hw + c * dh:h * hw + (c + 1) * dh] for h in range(DIFF_HEADS)]
                                  + [zpad_k], axis=0).astype(BF16)
            s_parts.append(_dot_nt(qc, knc))
        s = jnp.concatenate(s_parts, axis=0)
        r = lax.broadcasted_iota(jnp.int32, s.shape, 0)
        col = lax.broadcasted_iota(jnp.int32, s.shape, 1)
        visible = jnp.logical_and(jnp.right_shift(col, tok_bits) == row_head(r),
                                  jnp.bitwise_and(col, n_tok - 1) <= jnp.bitwise_and(r, n_tok - 1))
        s = jnp.where(visible, s, -jnp.inf)
        m = jnp.max(s, axis=-1, keepdims=True)
        p = jnp.exp(s - m)
        m_scr[...] = jnp.broadcast_to(m, m_scr.shape)
        l_scr[...] = jnp.broadcast_to(jnp.sum(p, axis=-1, keepdims=True), l_scr.shape)
        vn_rows = jnp.concatenate([vn[:, h * hw:(h + 1) * hw] for h in range(DIFF_HEADS)]
                                  + [jnp.zeros((PAGE_SIZE - n_half, hw), F32)], axis=0).astype(BF16)
        acc_scr[...] = _dot(p.astype(BF16), vn_rows)

    q0 = q_scr[0]
    q1 = q_scr[1]

    def page_scores(kp_ref):
        k0 = kp_ref[0, pl.ds(0, page_rows, stride=2), :].astype(BF16)
        k1 = kp_ref[0, pl.ds(1, page_rows, stride=2), :].astype(BF16)
        return jnp.concatenate([_dot_nt(q0, k0), _dot_nt(q1, k1)], axis=0)

    r = lax.broadcasted_iota(jnp.int32, (2 * n_half, page_rows), 0)
    col = lax.broadcasted_iota(jnp.int32, (2 * n_half, page_rows), 1)
    own_head = jnp.bitwise_and(col, DIFF_HEADS - 1) == row_head(r)
    m, l, acc = m_scr[...], l_scr[...], acc_scr[...]
    s_next = page_scores(kp_refs[0])
    for i in range(pages_per_step):
        s = jnp.where(own_head, s_next, -jnp.inf)
        if i + 1 < pages_per_step:
            s_next = page_scores(kp_refs[i + 1])
        m_new = jnp.maximum(m, jnp.max(s, axis=-1, keepdims=True))
        corr = jnp.exp(m - m_new)
        p = jnp.exp(s - _lane_tile(m_new, page_rows // LANES))
        l = l * corr + jnp.sum(p, axis=-1, keepdims=True)
        acc = acc * _lane_tile(corr, DIFF_VDIM // LANES) + _dot(p.astype(BF16), vp_refs[i][0].astype(BF16))
        m = m_new
    m_scr[...] = m
    l_scr[...] = l
    acc_scr[...] = acc

    @pl.when(step == pl.num_programs(1) - 1)
    def _():
        lam = _lambda(lq1_ref, lk1_ref, lq2_ref, lk2_ref, lam_init)
        o_all = acc_scr[...] / l_scr[...][:, :1]
        dn = dn_ref[...]
        outs = []
        for h in range(DIFF_HEADS):
            o = o_all[h * n_tok:(h + 1) * n_tok] - lam * o_all[n_half + h * n_tok:n_half + (h + 1) * n_tok]
            outs.append(_rms(o, dn) * (1.0 - lam_init))
        o_ref[...] = jnp.concatenate(outs, axis=1).astype(o_ref.dtype)


def _diff_sample(zq, zk, zv, cache_k, cache_v, page_table, lams, diff_norm, *, n_tok, lam_init,
                 pages_per_step=8):
    t = zq.shape[0]
    batch, n_pages = page_table.shape
    assert n_pages % pages_per_step == 0 and t == batch * n_tok
    assert n_tok & (n_tok - 1) == 0 and n_tok % 8 == 0 and DIFF_HEADS * n_tok <= PAGE_SIZE
    n_steps = n_pages // pages_per_step
    dh = DIFF_HEAD_DIM
    hw = 2 * dh
    n_half = DIFF_HEADS * n_tok
    n_rows = 2 * n_half
    page_rows = PAGE_SIZE * DIFF_HEADS
    tok_spec = pl.BlockSpec((n_tok, DIFF_WIDTH), lambda b, s, pt: (b, 0))

    def page_spec(shape, i):
        return pl.BlockSpec((1,) + shape,
                            lambda b, s, pt: (pt[b * n_pages + s * pages_per_step + i], 0, 0))

    vec = pl.BlockSpec((1, dh), lambda b, s, pt: (0, 0))
    in_specs = ([tok_spec, tok_spec, tok_spec]
                + [page_spec((2 * page_rows, dh), i) for i in range(pages_per_step)]
                + [page_spec((page_rows, hw), i) for i in range(pages_per_step)]
                + [vec, vec, vec, vec, pl.BlockSpec((1, DIFF_VDIM), lambda b, s, pt: (0, 0))])
    page_bytes = page_rows * hw * 4
    vmem = 4 * pages_per_step * page_bytes + 2 * pages_per_step * page_bytes \
        + 6 * n_rows * pages_per_step * page_rows * 4 + 4 * 1024 * 1024
    body = functools.partial(_diff_sample_body, n_tok=n_tok, pages_per_step=pages_per_step,
                             lam_init=lam_init)
    grid_spec = pltpu.PrefetchScalarGridSpec(
        num_scalar_prefetch=1,
        grid=(batch, n_steps),
        in_specs=in_specs,
        out_specs=pl.BlockSpec((n_tok, DIFF_WIDTH), lambda b, s, pt: (b, 0)),
        scratch_shapes=[pltpu.VMEM((2, n_half, dh), BF16), pltpu.VMEM((n_rows, LANES), F32),
                        pltpu.VMEM((n_rows, LANES), F32), pltpu.VMEM((n_rows, DIFF_VDIM), F32)],
    )
    return pl.pallas_call(
        body,
        grid_spec=grid_spec,
        out_shape=jax.ShapeDtypeStruct((t, DIFF_WIDTH), F32),
        compiler_params=_cparams(("parallel", "arbitrary"), vmem),
        name="diff_attn_sample",
    )(page_table.reshape(-1), zq, zk, zv, *([cache_k] * pages_per_step), *([cache_v] * pages_per_step),
      *lams, diff_norm.reshape(1, -1))


def _layer(x, p, lam_init, *, batch, seq, gla_state, attend):
    h = _rmsnorm(x, p['pre_mix_norm'], BF16)
    wt = p['w_in_t']
    zg = _matmul_nt(h, wt, row0=0, n_out=GLA_COLS, tn=512, out_dtype=BF16)
    zga = _matmul_nt(h, wt, row0=GATE_COL0, n_out=LANES, tn=LANES, out_dtype=F32)
    zq = _matmul_nt(h, wt, row0=DIFF_COL0, n_out=DIFF_WIDTH, tn=512,
                    out_dtype=BF16 if seq % 16 == 0 else F32)
    zk, k_rows = _matmul_nt_keys(h, wt, row0=DIFF_COL0 + DIFF_WIDTH, n_out=DIFF_WIDTH, tn=512)
    zv = _matmul_nt(h, wt, row0=DIFF_COL0 + 2 * DIFF_WIDTH, n_out=DIFF_WIDTH, tn=512, out_dtype=F32)
    if gla_state is None:
        o_gla, s_out = _gla(zg, zga, p['w_a2p'], p['b_a'], p['gla_norm'], None, batch=batch, seq=seq,
                            c_valid=GLA_CHUNK, n_chunks=8, n_seqs=1)
    else:
        o_gla, s_out = _gla(zg, zga, p['w_a2p'], p['b_a'], p['gla_norm'], gla_state, batch=batch,
                            seq=seq, c_valid=seq, n_chunks=1, n_seqs=8)
    o_diff = attend(zq, zk, zv)
    mixed = _matmul2(o_gla, o_diff, p['w_o'], tn=512)
    x1, h2 = _residual_norm2(x, mixed, p['post_mix_norm'], p['pre_ffn_norm'])
    act = _gate_up(h2, p['w_gate'], p['w_up'], tn=256)
    f = _matmul(act, p['w_down'], col0=0, n_out=D_MODEL, tn=256, out_dtype=F32)
    y = _residual_norm(x1, f, p['post_ffn_norm'])
    return y, k_rows, zv, s_out


def kernel(x_prompt, x_sample, cache_k, cache_v, state_gla, page_table, pre_mix_norm, w_in, w_a2, b_a, gla_norm, lambda_q1, lambda_k1, lambda_q2, lambda_k2, diff_norm, w_o, post_mix_norm, pre_ffn_norm, w_gate, w_up, w_down, post_ffn_norm):
    depth = w_in.shape[0]
    bp, sp, d = x_prompt.shape
    bs, ss, _ = x_sample.shape
    n_pool = cache_k.shape[1]
    xp = x_prompt.reshape(bp * sp, d)
    xs = x_sample.reshape(bs * ss, d)
    outs = [[] for _ in range(6)]
    for l in range(depth):
        lam_init = 0.8 - 0.6 * math.exp(-0.3 * l)
        p = {
            'pre_mix_norm': pre_mix_norm[l], 'w_in_t': jnp.swapaxes(w_in[l], 0, 1),
            'w_a2p': jnp.pad(w_a2[l], ((0, LANES - GLA_GATE_RANK), (0, 0))),
            'b_a': b_a[l], 'gla_norm': gla_norm[l], 'w_o': w_o[l],
            'post_mix_norm': post_mix_norm[l], 'pre_ffn_norm': pre_ffn_norm[l],
            'w_gate': w_gate[l], 'w_up': w_up[l], 'w_down': w_down[l],
            'post_ffn_norm': post_ffn_norm[l],
        }
        lams = [v[l].reshape(1, -1) for v in (lambda_q1, lambda_k1, lambda_q2, lambda_k2)]
        ck = cache_k[l].reshape(n_pool, 2 * PAGE_SIZE * DIFF_HEADS, DIFF_HEAD_DIM)
        cv = cache_v[l].reshape(n_pool, PAGE_SIZE * DIFF_HEADS, DIFF_VDIM)
        attend_p = functools.partial(_diff_prompt, lams=lams, diff_norm=diff_norm[l], batch=bp, seq=sp,
                                     lam_init=lam_init)
        attend_s = functools.partial(_diff_sample, cache_k=ck, cache_v=cv, page_table=page_table,
                                     lams=lams, diff_norm=diff_norm[l], n_tok=ss, lam_init=lam_init)
        xp, kp, vp, s_p = _layer(xp, p, lam_init, batch=bp, seq=sp, gla_state=None, attend=attend_p)
        xs, ks, vs, s_s = _layer(xs, p, lam_init, batch=bs, seq=ss, gla_state=state_gla[l],
                                 attend=attend_s)
        outs[0].append(kp.reshape(bp, sp, DIFF_HEADS, 2, DIFF_HEAD_DIM))
        outs[1].append(vp.reshape(bp, sp, DIFF_HEADS, DIFF_VDIM))
        outs[2].append(s_p)
        outs[3].append(ks.reshape(bs, ss, DIFF_HEADS, 2, DIFF_HEAD_DIM))
        outs[4].append(vs.reshape(bs, ss, DIFF_HEADS, DIFF_VDIM))
        outs[5].append(s_s)
    k_p, v_p, st_p, k_s, v_s, st_s = [o[0][None] if depth == 1 else jnp.stack(o) for o in outs]
    return (xp.reshape(bp, sp, d), xs.reshape(bs, ss, d), k_p, v_p, st_p, k_s, v_s, st_s)
```
